```python
import math
import jax, jax.numpy as jnp
from jax import lax
import numpy as np

D_MODEL = 1024
BATCH = 2
SEQ = 16384
DEPTH = 2
DEC_BATCH = 8
DEC_SEQ = 64
PAST_LEN = 2048

CHUNK = 64
Q_BLOCK = 128
MEM_LEN = 256
ROPE_THETA = 10000.0
EPS = 1e-6
N_RET = (DEPTH + 1) // 2
N_DIFF = DEPTH // 2
RET_HEADS = 4
RET_DK = D_MODEL // RET_HEADS
RET_DV = 2 * RET_DK
DIFF_HEADS = D_MODEL // 128
DIFF_HD = 64
CROSS_HEADS = 4
CROSS_HD = D_MODEL // CROSS_HEADS
D_FF = 2816
N_EXPERTS = 8
TOP_K = 2
E_FF = 3584
MOE_BLOCK = 128

kernel_name = "retnet_diffattn_stream_step"


def rms_normalize(x):
    x32 = x.astype(jnp.float32)
    return (x32 * lax.rsqrt(jnp.mean(x32 * x32, axis=-1, keepdims=True) + EPS)).astype(x.dtype)


def rmsnorm(x, g):
    return rms_normalize(x) * g.astype(x.dtype)


def rope(x, pos):
    d = x.shape[-1]
    half = d // 2
    inv = ROPE_THETA ** (-jnp.arange(half, dtype=jnp.float32) / half)
    ang = pos.astype(jnp.float32)[:, None] * inv[None, :]
    shape = (1, x.shape[1]) + (1,) * (x.ndim - 3) + (half,)
    cos = jnp.cos(ang).reshape(shape).astype(x.dtype)
    sin = jnp.sin(ang).reshape(shape).astype(x.dtype)
    x1, x2 = x[..., :half], x[..., half:]
    return jnp.concatenate([x1 * cos - x2 * sin, x2 * cos + x1 * sin], axis=-1)


def retention_chunks(q, k, v, s0, chunk):
    B, T, H, DK = q.shape
    DV = v.shape[-1]
    nc = T // chunk
    log_g = jnp.log1p(-(2.0 ** (-5.0 - jnp.arange(H, dtype=jnp.float32))))
    i = jnp.arange(chunk, dtype=jnp.float32)
    intra = jnp.exp(jnp.abs(i[:, None] - i[None, :])[None] * log_g[:, None, None]).astype(q.dtype)
    q_dec = jnp.exp((i[:, None] + 1.0) * log_g[None, :]).astype(q.dtype)
    k_dec = jnp.exp((chunk - 1.0 - i)[:, None] * log_g[None, :]).astype(q.dtype)
    c_dec = jnp.exp(chunk * log_g).astype(q.dtype)
    qc = q.reshape(B, nc, chunk, H, DK)
    kc = k.reshape(B, nc, chunk, H, DK)
    vc = v.reshape(B, nc, chunk, H, DV)
    scores = jnp.einsum('bnihd,bnjhd->bnhij', qc, kc) * intra
    o_intra = jnp.einsum('bnhij,bnjhe->bnihe', scores, vc)

    def step(s, blk):
        qn, kn, vn = blk
        o_cross = jnp.einsum('bihd,bhde->bihe', qn * q_dec[None, :, :, None], s)
        s = s * c_dec[None, :, None, None] + jnp.einsum('bjhd,bjhe->bhde', kn * k_dec[None, :, :, None], vn)
        return s, o_cross

    s_end, o_cross = lax.scan(step, s0, (jnp.moveaxis(qc, 1, 0), jnp.moveaxis(kc, 1, 0), jnp.moveaxis(vc, 1, 0)))
    o = o_intra + jnp.moveaxis(o_cross, 0, 1)
    return o.reshape(B, T, H, DV), s_end


def retention_layer(h, pos, s0, w_in, w_out):
    B, T, _ = h.shape
    proj = h @ w_in
    q, k, v, g = jnp.split(proj, [RET_HEADS * RET_DK, 2 * RET_HEADS * RET_DK,
                                  2 * RET_HEADS * RET_DK + RET_HEADS * RET_DV], axis=-1)
    q = rope(q.reshape(B, T, RET_HEADS, RET_DK), pos)
    k = rope(k.reshape(B, T, RET_HEADS, RET_DK), pos) * RET_DK ** -0.5
    v = v.reshape(B, T, RET_HEADS, RET_DV)
    o, s_new = retention_chunks(q, k, v, s0, min(CHUNK, T))
    o = rms_normalize(o).reshape(B, T, RET_HEADS * RET_DV)
    return (jax.nn.silu(g) * o) @ w_out, s_new


def diff_attend(q, k, v, q_pos, k_pos, lam):
    s = jnp.einsum('bqhmd,bkhmd->bhmqk', q, k, preferred_element_type=jnp.float32)
    mask = (k_pos[None, :] // CHUNK) <= (q_pos[:, None] // CHUNK)
    s = jnp.where(mask[None, None, None], s, jnp.finfo(jnp.float32).min)
    p = jax.nn.softmax(s, axis=-1)
    a = (p[:, :, 0] - lam * p[:, :, 1]).astype(v.dtype)
    return jnp.einsum('bhqk,bkhe->bqhe', a, v)


def diff_attention(q, k, v, q_pos, k_pos, lam):
    B, T = q.shape[0], q.shape[1]
    if T <= Q_BLOCK:
        return diff_attend(q, k, v, q_pos, k_pos, lam)
    nb = T // Q_BLOCK
    qb = jnp.moveaxis(q.reshape((B, nb, Q_BLOCK) + q.shape[2:]), 1, 0)
    pb = q_pos.reshape(nb, Q_BLOCK)
    out = lax.map(lambda a: diff_attend(a[0], k, v, a[1], k_pos, lam), (qb, pb))
    return jnp.moveaxis(out, 0, 1).reshape(B, T, DIFF_HEADS, 2 * DIFF_HD)


def diff_layer(h, pos, past_k, past_v, w_in, lam_p, subln, w_out, lam_init):
    B, T, _ = h.shape
    q, k, v = jnp.split(h @ w_in, 3, axis=-1)
    q = rope(q.reshape(B, T, DIFF_HEADS, 2, DIFF_HD), pos) * DIFF_HD ** -0.5
    k_rows = rope(k.reshape(B, T, DIFF_HEADS, 2, DIFF_HD), pos).reshape(B, T, DIFF_HEADS, 2 * DIFF_HD)
    v_rows = v.reshape(B, T, DIFF_HEADS, 2 * DIFF_HD)
    if past_k is None:
        k_all, v_all = k_rows, v_rows
    else:
        k_all = jnp.concatenate([past_k, k_rows], axis=1)
        v_all = jnp.concatenate([past_v, v_rows], axis=1)
    n_keys = k_all.shape[1]
    k_pos = jnp.arange(n_keys, dtype=jnp.int32)
    lp = lam_p.astype(jnp.float32)
    lam = jnp.exp(jnp.sum(lp[0] * lp[1])) - jnp.exp(jnp.sum(lp[2] * lp[3])) + lam_init
    o = diff_attention(q, k_all.reshape(B, n_keys, DIFF_HEADS, 2, DIFF_HD), v_all, pos, k_pos, lam)
    o = rmsnorm(o, subln) * (1.0 - lam_init)
    return o.reshape(B, T, DIFF_HEADS * 2 * DIFF_HD) @ w_out, k_rows, v_rows


def memory_kv(mem, norm_mem, cross_wkv):
    B, M, _ = mem.shape
    ks, vs = [], []
    for i in range(DEPTH):
        mk, mv = jnp.split(rmsnorm(mem, norm_mem[i]) @ cross_wkv[i], 2, axis=-1)
        ks.append(mk.reshape(B, M, CROSS_HEADS, CROSS_HD))
        vs.append(mv.reshape(B, M, CROSS_HEADS, CROSS_HD))
    return jnp.stack(ks), jnp.stack(vs)


def cross_attention(h, mk, mv, w_q, w_o):
    B, T, _ = h.shape
    q = (h @ w_q).reshape(B, T, CROSS_HEADS, CROSS_HD) * CROSS_HD ** -0.5
    s = jnp.einsum('bqhd,bkhd->bhqk', q, mk, preferred_element_type=jnp.float32)
    p = jax.nn.softmax(s, axis=-1).astype(mv.dtype)
    o = jnp.einsum('bhqk,bkhd->bqhd', p, mv)
    return o.reshape(B, T, D_MODEL) @ w_o


def dense_swiglu(h, w_gu, w_down):
    g, u = jnp.split(h @ w_gu, 2, axis=-1)
    return (jax.nn.silu(g) * u) @ w_down


def moe_swiglu(h, w_router, w_gu, w_down):
    B, T, D = h.shape
    n = B * T
    a = n * TOP_K
    xt = h.reshape(n, D)
    logits = jnp.einsum('nd,de->ne', xt, w_router, preferred_element_type=jnp.float32)
    top_val, top_idx = lax.top_k(logits, TOP_K)
    top_w = jax.nn.softmax(top_val, axis=-1)
    e_flat = top_idx.reshape(a)
    tok_flat = jnp.arange(a, dtype=jnp.int32) // TOP_K
    w_flat = top_w.reshape(a)
    order = jnp.argsort(e_flat)
    e_sorted = e_flat[order]
    counts = jnp.bincount(e_flat, length=N_EXPERTS)
    start = jnp.cumsum(counts) - counts
    padded = (counts + MOE_BLOCK - 1) // MOE_BLOCK * MOE_BLOCK
    pad_end = jnp.cumsum(padded)
    pad_start = pad_end - padded
    dest = pad_start[e_sorted] + (jnp.arange(a, dtype=jnp.int32) - start[e_sorted])
    n_blocks = -(-a // MOE_BLOCK) + N_EXPERTS
    rows = n_blocks * MOE_BLOCK
    row_tok = jnp.zeros((rows,), jnp.int32).at[dest].set(tok_flat[order])
    row_w = jnp.zeros((rows,), w_flat.dtype).at[dest].set(w_flat[order])
    block_start = jnp.arange(n_blocks, dtype=pad_end.dtype) * MOE_BLOCK
    block_e = jnp.minimum(jnp.searchsorted(pad_end, block_start, side='right'), N_EXPERTS - 1)
    xb = xt[row_tok].reshape(n_blocks, MOE_BLOCK, D)

    def expert_block(args):
        xe, e = args
        g, u = jnp.split(xe @ w_gu[e], 2, axis=-1)
        return (jax.nn.silu(g) * u) @ w_down[e]

    yb = lax.map(expert_block, (xb, block_e)).reshape(rows, D)
    y = jax.ops.segment_sum(yb * row_w[:, None].astype(yb.dtype), row_tok, num_segments=n)
    return y.reshape(B, T, D)


def trunk(x, pos, ret_s0, past_k, past_v, mem_k, mem_v,
          norm_mix, norm_cross, norm_ffn, norm_final,
          ret_w_in, ret_w_out, diff_w_in, diff_lambda, diff_subln, diff_w_out,
          cross_wq, cross_wo, ffn_w_gu, ffn_w_down, moe_router, moe_w_gu, moe_w_down):
    ret_states, new_k, new_v = [], [], []
    for i in range(DEPTH):
        j = i // 2
        h = rmsnorm(x, norm_mix[i])
        if i % 2 == 0:
            o, s_new = retention_layer(h, pos, ret_s0[j], ret_w_in[j], ret_w_out[j])
            ret_states.append(s_new)
        else:
            lam_init = 0.8 - 0.6 * math.exp(-0.3 * i)
            o, k_rows, v_rows = diff_layer(h, pos,
                                           None if past_k is None else past_k[j],
                                           None if past_v is None else past_v[j],
                                           diff_w_in[j], diff_lambda[j], diff_subln[j], diff_w_out[j], lam_init)
            new_k.append(k_rows)
            new_v.append(v_rows)
        x = x + o
        x = x + cross_attention(rmsnorm(x, norm_cross[i]), mem_k[i], mem_v[i], cross_wq[i], cross_wo[i])
        h = rmsnorm(x, norm_ffn[i])
        if i % 2 == 0:
            x = x + dense_swiglu(h, ffn_w_gu[j], ffn_w_down[j])
        else:
            x = x + moe_swiglu(h, moe_router[j], moe_w_gu[j], moe_w_down[j])
    return rmsnorm(x, norm_final), jnp.stack(ret_states), jnp.stack(new_k), jnp.stack(new_v)


def setup_inputs(seed: int = 0) -> dict:
    key = jax.random.key(seed)
    ks = jax.random.split(key, 32)

    def nrm(k, shape, scale):
        return jax.random.normal(k, shape, jnp.float32) * scale

    def gain(k, shape):
        return 1.0 + 0.05 * jax.random.normal(k, shape, jnp.float32)

    d = D_MODEL
    ret_in = 2 * RET_HEADS * RET_DK + 2 * RET_HEADS * RET_DV
    return {
        "x_prompt": nrm(ks[0], (BATCH, SEQ, d), 1.0),
        "x_sample": nrm(ks[1], (DEC_BATCH, DEC_SEQ, d), 1.0),
        "state_ret": nrm(ks[2], (N_RET, DEC_BATCH, RET_HEADS, RET_DK, RET_DV), 0.5),
        "cache_diff_k": nrm(ks[3], (N_DIFF, DEC_BATCH, PAST_LEN, DIFF_HEADS, 2 * DIFF_HD), 1.0),
        "cache_diff_v": nrm(ks[4], (N_DIFF, DEC_BATCH, PAST_LEN, DIFF_HEADS, 2 * DIFF_HD), 1.0),
        "cache_mem_k": nrm(ks[5], (DEPTH, DEC_BATCH, MEM_LEN, CROSS_HEADS, CROSS_HD), 1.0),
        "cache_mem_v": nrm(ks[6], (DEPTH, DEC_BATCH, MEM_LEN, CROSS_HEADS, CROSS_HD), 1.0),
        "mem_prompt": nrm(ks[7], (BATCH, MEM_LEN, d), 1.0),
        "norm_mix": gain(ks[8], (DEPTH, d)),
        "norm_cross": gain(ks[9], (DEPTH, d)),
        "norm_mem": gain(ks[10], (DEPTH, d)),
        "norm_ffn": gain(ks[11], (DEPTH, d)),
        "norm_final": gain(ks[12], (d,)),
        "ret_w_in": nrm(ks[13], (N_RET, d, ret_in), d ** -0.5),
        "ret_w_out": nrm(ks[14], (N_RET, RET_HEADS * RET_DV, d), (RET_HEADS * RET_DV) ** -0.5),
        "diff_w_in": nrm(ks[15], (N_DIFF, d, 3 * DIFF_HEADS * 2 * DIFF_HD), d ** -0.5),
        "diff_lambda": nrm(ks[16], (N_DIFF, 4, DIFF_HD), 0.1),
        "diff_subln": gain(ks[17], (N_DIFF, 2 * DIFF_HD)),
        "diff_w_out": nrm(ks[18], (N_DIFF, DIFF_HEADS * 2 * DIFF_HD, d), (DIFF_HEADS * 2 * DIFF_HD) ** -0.5),
        "cross_wq": nrm(ks[19], (DEPTH, d, d), d ** -0.5),
        "cross_wkv": nrm(ks[20], (DEPTH, d, 2 * d), d ** -0.5),
        "cross_wo": nrm(ks[21], (DEPTH, d, d), d ** -0.5),
        "ffn_w_gu": nrm(ks[22], (N_RET, d, 2 * D_FF), d ** -0.5),
        "ffn_w_down": nrm(ks[23], (N_RET, D_FF, d), D_FF ** -0.5),
        "moe_router": nrm(ks[24], (N_DIFF, d, N_EXPERTS), d ** -0.5),
        "moe_w_gu": nrm(ks[25], (N_DIFF, N_EXPERTS, d, 2 * E_FF), d ** -0.5),
        "moe_w_down": nrm(ks[26], (N_DIFF, N_EXPERTS, E_FF, d), E_FF ** -0.5),
    }


def reference(x_prompt, x_sample, state_ret, cache_diff_k, cache_diff_v, cache_mem_k, cache_mem_v, mem_prompt,
              norm_mix, norm_cross, norm_mem, norm_ffn, norm_final,
              ret_w_in, ret_w_out, diff_w_in, diff_lambda, diff_subln, diff_w_out,
              cross_wq, cross_wkv, cross_wo, ffn_w_gu, ffn_w_down, moe_router, moe_w_gu, moe_w_down):
    weights = (norm_mix, norm_cross, norm_ffn, norm_final,
               ret_w_in, ret_w_out, diff_w_in, diff_lambda, diff_subln, diff_w_out,
               cross_wq, cross_wo, ffn_w_gu, ffn_w_down, moe_router, moe_w_gu, moe_w_down)
    b_p, t_p = x_prompt.shape[0], x_prompt.shape[1]
    mem_k_p, mem_v_p = memory_kv(mem_prompt, norm_mem, cross_wkv)
    pos_p = jnp.arange(t_p, dtype=jnp.int32)
    s0 = jnp.zeros((N_RET, b_p, RET_HEADS, RET_DK, RET_DV), x_prompt.dtype)
    y_prompt, ret_state_p, diff_k_p, diff_v_p = trunk(x_prompt, pos_p, s0, None, None, mem_k_p, mem_v_p, *weights)
    n_past = cache_diff_k.shape[2]
    pos_s = n_past + jnp.arange(x_sample.shape[1], dtype=jnp.int32)
    y_sample, ret_state_s, diff_k_s, diff_v_s = trunk(x_sample, pos_s, state_ret, cache_diff_k, cache_diff_v,
                                                      cache_mem_k, cache_mem_v, *weights)
    return (y_prompt, y_sample, ret_state_p, diff_k_p, diff_v_p, mem_k_p, mem_v_p, ret_state_s, diff_k_s, diff_v_s)
```

```python
import functools
import math

import jax
import jax.numpy as jnp
from jax import lax
from jax.experimental import pallas as pl
from jax.experimental.pallas import tpu as pltpu

_CHUNK = 64
_ROPE_THETA = 10000.0
_EPS = 1e-6
_RET_HEADS = 4
_DIFF_HD = 64
_CROSS_HEADS = 4
_N_EXPERTS = 8
_TOP_K = 2

_VMEM_LIMIT_V7X = 56 * 1024 * 1024
_LANES = 128

_F32 = jnp.float32
_BF16 = jnp.bfloat16
_NT = (((1,), (1,)), ((), ()))
_TN = (((0,), (0,)), ((), ()))
_NEG = -1e30


def _params(*sem):
    return pltpu.CompilerParams(dimension_semantics=sem, vmem_limit_bytes=_VMEM_LIMIT_V7X)


def _tile(n, pref):
    t = min(n, pref)
    assert n % t == 0, (n, pref)
    return t


def _rms(x):
    return x * lax.rsqrt(jnp.mean(x * x, axis=-1, keepdims=True) + _EPS)


def _silu(x):
    return x * jax.nn.sigmoid(x)


def _proj_kernel(x_ref, g_ref, w_ref, cos_ref, sin_ref, o_ref, xn_ref, *, half, n_rope, n_first, scale_first,
                 scale_rest):
    j = pl.program_id(1)

    @pl.when(j == 0)
    def _():
        xn_ref[...] = (_rms(x_ref[...]) * g_ref[...]).astype(_BF16)

    acc = jnp.dot(xn_ref[...], w_ref[...], preferred_element_type=_F32)
    tn = acc.shape[1]

    if n_rope > 0:
        @pl.when(j < n_rope)
        def _():
            cos = cos_ref[...]
            sin = sin_ref[...]
            pieces = []
            if half == _LANES:
                for c in range(tn // (2 * _LANES)):
                    x1 = acc[:, 2 * c * _LANES:(2 * c + 1) * _LANES]
                    x2 = acc[:, (2 * c + 1) * _LANES:(2 * c + 2) * _LANES]
                    pieces += [x1 * cos - x2 * sin, x2 * cos + x1 * sin]
            else:
                lane = lax.broadcasted_iota(jnp.int32, (1, _LANES), 1)
                first = (lane % (2 * half)) < half
                for c in range(tn // _LANES):
                    xc = acc[:, c * _LANES:(c + 1) * _LANES]
                    rot = jnp.where(first, pltpu.roll(xc, _LANES - half, 1), pltpu.roll(xc, half, 1))
                    pieces.append(xc * cos + rot * sin)
            r = jnp.concatenate(pieces, axis=1)
            scale = jnp.where(j < n_first, scale_first, scale_rest).astype(_F32)
            o_ref[...] = (r * scale).astype(o_ref.dtype)

        @pl.when(j >= n_rope)
        def _():
            o_ref[...] = acc.astype(o_ref.dtype)
    else:
        o_ref[...] = acc.astype(o_ref.dtype)


def _proj(x, gain, w, cos, sin, *, out_dtype, half=_LANES, n_rope_cols=0, n_first_cols=0, scale_first=1.0,
          scale_rest=1.0, tm=512, tn=512):
    n, d = x.shape
    d_out = w.shape[1]
    tm = _tile(n, tm)
    tn = _tile(d_out, tn)
    assert n_rope_cols % tn == 0 and n_first_cols % tn == 0
    if cos is None:
        cos = jnp.zeros((tm, _LANES), _F32)
        sin = cos
    r_tiles = cos.shape[0] // tm
    assert cos.shape[0] % tm == 0
    kern = functools.partial(_proj_kernel, half=half, n_rope=n_rope_cols // tn, n_first=n_first_cols // tn,
                             scale_first=scale_first, scale_rest=scale_rest)
    return pl.pallas_call(
        kern,
        grid=(n // tm, d_out // tn),
        in_specs=[
            pl.BlockSpec((tm, d), lambda i, j: (i, 0)),
            pl.BlockSpec((1, d), lambda i, j: (0, 0)),
            pl.BlockSpec((d, tn), lambda i, j: (0, j)),
            pl.BlockSpec((tm, _LANES), lambda i, j: (i % r_tiles, 0)),
            pl.BlockSpec((tm, _LANES), lambda i, j: (i % r_tiles, 0)),
        ],
        out_specs=pl.BlockSpec((tm, tn), lambda i, j: (i, j)),
        out_shape=jax.ShapeDtypeStruct((n, d_out), out_dtype),
        scratch_shapes=[pltpu.VMEM((tm, d), _BF16)],
        compiler_params=_params("parallel", "arbitrary"),
        name="norm_proj",
    )(x, gain.reshape(1, d), w, cos, sin)


def _out_proj_kernel(a_ref, w_ref, r_ref, o_ref):
    o_ref[...] = r_ref[...] + jnp.dot(a_ref[...], w_ref[...], preferred_element_type=_F32)


def _out_proj(a, w, res, *, tm=512):
    n, k = a.shape
    d = w.shape[1]
    tm = _tile(n, tm)
    return pl.pallas_call(
        _out_proj_kernel,
        grid=(n // tm,),
        in_specs=[
            pl.BlockSpec((tm, k), lambda i: (i, 0)),
            pl.BlockSpec((k, d), lambda i: (0, 0)),
            pl.BlockSpec((tm, d), lambda i: (i, 0)),
        ],
        out_specs=pl.BlockSpec((tm, d), lambda i: (i, 0)),
        out_shape=jax.ShapeDtypeStruct((n, d), _F32),
        compiler_params=_params("parallel"),
        name="out_proj_residual",
    )(a, w, res)


def _retention_kernel(q_ref, k_ref, v_ref, g_ref, s0_ref, dm_ref, qd_ref, kd_ref, cd_ref, o_ref, so_ref, s_ref):
    n = pl.program_id(2)

    @pl.when(n == 0)
    def _():
        s_ref[...] = s0_ref[0, 0]

    q = q_ref[...]
    k = k_ref[...]
    v = v_ref[...]
    sc = lax.dot_general(q, k, _NT, preferred_element_type=_F32)
    p = (sc * dm_ref[0]).astype(_BF16)
    s_old = s_ref[...]
    q_dec = (q.astype(_F32) * qd_ref[0]).astype(_BF16)
    o = jnp.dot(p, v, preferred_element_type=_F32) + jnp.dot(q_dec, s_old.astype(_BF16),
                                                            preferred_element_type=_F32)
    k_dec = (k.astype(_F32) * kd_ref[0]).astype(_BF16)
    s_ref[...] = s_old * cd_ref[0] + lax.dot_general(k_dec, v, _TN, preferred_element_type=_F32)
    g = g_ref[...].astype(_F32)
    o_ref[...] = (_silu(g) * _rms(o)).astype(o_ref.dtype)

    @pl.when(n == pl.num_programs(2) - 1)
    def _():
        so_ref[0, 0] = s_ref[...]


def _retention(qkvg, s0, batch, t, *, block=256):
    h = _RET_HEADS
    dk, dv = s0.shape[2], s0.shape[3]
    lb = _tile(t, block)
    nb = t // lb
    assert lb % _CHUNK == 0
    log_g = jnp.log1p(-(2.0 ** (-5.0 - jnp.arange(h, dtype=_F32))))
    i = jnp.arange(lb, dtype=_F32)
    ci = jnp.arange(lb) // _CHUNK
    diff = i[:, None] - i[None, :]
    same = ci[:, None] == ci[None, :]
    before = ci[None, :] < ci[:, None]
    expo = jnp.where(same, jnp.abs(diff), diff)
    dmask = jnp.where((same | before)[None], jnp.exp(expo[None] * log_g[:, None, None]), 0.0).astype(_F32)
    q_dec = jnp.exp((i[None, :] + 1.0) * log_g[:, None])[..., None]
    k_dec = jnp.exp((lb - 1.0 - i)[None, :] * log_g[:, None])[..., None]
    c_dec = jnp.exp(lb * log_g).reshape(h, 1, 1)
    kq, kv = dk, dv
    q_blocks = (h * dk) // kq
    v_off = (2 * h * dk) // kv
    g_off = v_off + h
    return pl.pallas_call(
        _retention_kernel,
        grid=(batch, h, nb),
        in_specs=[
            pl.BlockSpec((lb, kq), lambda b, hh, n: (b * nb + n, hh)),
            pl.BlockSpec((lb, kq), lambda b, hh, n: (b * nb + n, q_blocks + hh)),
            pl.BlockSpec((lb, kv), lambda b, hh, n: (b * nb + n, v_off + hh)),
            pl.BlockSpec((lb, kv), lambda b, hh, n: (b * nb + n, g_off + hh)),
            pl.BlockSpec((1, 1, dk, dv), lambda b, hh, n: (b, hh, 0, 0)),
            pl.BlockSpec((1, lb, lb), lambda b, hh, n: (hh, 0, 0)),
            pl.BlockSpec((1, lb, 1), lambda b, hh, n: (hh, 0, 0)),
            pl.BlockSpec((1, lb, 1), lambda b, hh, n: (hh, 0, 0)),
            pl.BlockSpec((1, 1, 1), lambda b, hh, n: (hh, 0, 0)),
        ],
        out_specs=[
            pl.BlockSpec((lb, kv), lambda b, hh, n: (b * nb + n, hh)),
            pl.BlockSpec((1, 1, dk, dv), lambda b, hh, n: (b, hh, 0, 0)),
        ],
        out_shape=[
            jax.ShapeDtypeStruct((batch * t, h * dv), _BF16),
            jax.ShapeDtypeStruct((batch, h, dk, dv), _F32),
        ],
        scratch_shapes=[pltpu.VMEM((dk, dv), _F32)],
        compiler_params=_params("parallel", "parallel", "arbitrary"),
        name="retention",
    )(qkvg, qkvg, qkvg, qkvg, s0, dmask, q_dec, k_dec, c_dec)


def _cross_kernel(x_ref, g_ref, wq_ref, wo_ref, mk_ref, mv_ref, o_ref, *, heads):
    x = x_ref[...]
    d = x.shape[1]
    hd = d // heads
    xn = (_rms(x) * g_ref[...]).astype(_BF16)
    q = (jnp.dot(xn, wq_ref[...], preferred_element_type=_F32) * (hd ** -0.5)).astype(_BF16)
    mk = mk_ref[0].astype(_BF16)
    mv = mv_ref[0].astype(_BF16)
    outs = []
    for hh in range(heads):
        sl = slice(hh * hd, (hh + 1) * hd)
        s = lax.dot_general(q[:, sl], mk[:, sl], _NT, preferred_element_type=_F32)
        s = s - jnp.max(s, axis=-1, keepdims=True)
        e = jnp.exp(s)
        p = (e / jnp.sum(e, axis=-1, keepdims=True)).astype(_BF16)
        outs.append(jnp.dot(p, mv[:, sl], preferred_element_type=_F32).astype(_BF16))
    o = jnp.concatenate(outs, axis=1)
    o_ref[...] = x + jnp.dot(o, wo_ref[...], preferred_element_type=_F32)


def _cross_attention(x, gain, wq, wo, mk, mv, batch, t, *, tm=512):
    n, d = x.shape
    tm = _tile(t, tm)
    nt = t // tm
    mem = mk.shape[1]
    return pl.pallas_call(
        functools.partial(_cross_kernel, heads=_CROSS_HEADS),
        grid=(batch, nt),
        in_specs=[
            pl.BlockSpec((tm, d), lambda b, i: (b * nt + i, 0)),
            pl.BlockSpec((1, d), lambda b, i: (0, 0)),
            pl.BlockSpec((d, d), lambda b, i: (0, 0)),
            pl.BlockSpec((d, d), lambda b, i: (0, 0)),
            pl.BlockSpec((1, mem, d), lambda b, i: (b, 0, 0)),
            pl.BlockSpec((1, mem, d), lambda b, i: (b, 0, 0)),
        ],
        out_specs=pl.BlockSpec((tm, d), lambda b, i: (b * nt + i, 0)),
        out_shape=jax.ShapeDtypeStruct((n, d), _F32),
        compiler_params=_params("parallel", "parallel"),
        name="cross_attention",
    )(x, gain.reshape(1, d), wq, wo, mk, mv)


def _ffn_kernel(x_ref, g_ref, wg_ref, wu_ref, wd_ref, o_ref, xn_ref, acc_ref):
    j = pl.program_id(1)

    @pl.when(j == 0)
    def _():
        xn_ref[...] = (_rms(x_ref[...]) * g_ref[...]).astype(_BF16)
        acc_ref[...] = x_ref[...]

    xn = xn_ref[...]
    gate = jnp.dot(xn, wg_ref[...], preferred_element_type=_F32)
    up = jnp.dot(xn, wu_ref[...], preferred_element_type=_F32)
    act = (_silu(gate) * up).astype(_BF16)
    acc_ref[...] += jnp.dot(act, wd_ref[...], preferred_element_type=_F32)

    @pl.when(j == pl.num_programs(1) - 1)
    def _():
        o_ref[...] = acc_ref[...]


def _dense_ffn(x, gain, w_gu, w_down, *, tm=512, tf=256):
    n, d = x.shape
    f = w_down.shape[0]
    tm = _tile(n, tm)
    tf = _tile(f, tf)
    nf = f // tf
    return pl.pallas_call(
        _ffn_kernel,
        grid=(n // tm, nf),
        in_specs=[
            pl.BlockSpec((tm, d), lambda i, j: (i, 0)),
            pl.BlockSpec((1, d), lambda i, j: (0, 0)),
            pl.BlockSpec((d, tf), lambda i, j: (0, j)),
            pl.BlockSpec((d, tf), lambda i, j: (0, nf + j)),
            pl.BlockSpec((tf, d), lambda i, j: (j, 0)),
        ],
        out_specs=pl.BlockSpec((tm, d), lambda i, j: (i, 0)),
        out_shape=jax.ShapeDtypeStruct((n, d), _F32),
        scratch_shapes=[pltpu.VMEM((tm, d), _BF16), pltpu.VMEM((tm, d), _F32)],
        compiler_params=_params("parallel", "arbitrary"),
        name="dense_swiglu",
    )(x, gain.reshape(1, d), w_gu, w_gu, w_down)


def _diff_attn_kernel(q_ref, k_ref, v_ref, lam_ref, sub_ref, o_ref, qm_ref, m_ref, l_ref, acc_ref, *, p0, tq, tk,
                      lam_init):
    qi = pl.program_id(2)
    ki = pl.program_id(3)
    hd = _DIFF_HD

    @pl.when(ki == 0)
    def _():
        q = q_ref[...]
        first = lax.broadcasted_iota(jnp.int32, (1, 2 * hd), 1) < hd
        qm_ref[0] = jnp.where(first, q, 0.0).astype(_BF16)
        qm_ref[1] = jnp.where(first, 0.0, q).astype(_BF16)
        m_ref[...] = jnp.full(m_ref.shape, _NEG, _F32)
        l_ref[...] = jnp.zeros(l_ref.shape, _F32)
        acc_ref[...] = jnp.zeros(acc_ref.shape, _F32)

    n_need = (p0 + (qi + 1) * tq + tk - 1) // tk

    @pl.when(ki < n_need)
    def _():
        k = k_ref[...].astype(_BF16)
        v = v_ref[...].astype(_BF16)
        q_chunk = (p0 + qi * tq + lax.broadcasted_iota(jnp.int32, (tq, 1), 0)) // _CHUNK
        k_chunk = (ki * tk + lax.broadcasted_iota(jnp.int32, (1, tk), 1)) // _CHUNK
        visible = k_chunk <= q_chunk
        for mm in range(2):
            s = lax.dot_general(qm_ref[mm], k, _NT, preferred_element_type=_F32)
            s = jnp.where(visible, s, _NEG)
            m_old = m_ref[mm]
            m_new = jnp.maximum(m_old, jnp.max(s, axis=-1, keepdims=True))
            alpha = jnp.exp(m_old - m_new)
            p = jnp.exp(s - m_new)
            l_ref[mm] = alpha * l_ref[mm] + jnp.sum(p, axis=-1, keepdims=True)
            acc_ref[mm] = alpha * acc_ref[mm] + jnp.dot(p.astype(_BF16), v, preferred_element_type=_F32)
            m_ref[mm] = m_new

    @pl.when(ki == pl.num_programs(3) - 1)
    def _():
        lp = lam_ref[...]
        lam = (jnp.exp(jnp.sum(lp[0:1] * lp[1:2], axis=-1, keepdims=True))
               - jnp.exp(jnp.sum(lp[2:3] * lp[3:4], axis=-1, keepdims=True)) + lam_init)
        o = acc_ref[0] / l_ref[0] - lam * (acc_ref[1] / l_ref[1])
        o_ref[...] = (_rms(o) * sub_ref[...] * (1.0 - lam_init)).astype(o_ref.dtype)


def _diff_attention(q_arr, k_arr, v_arr, q_col, k_col, v_col, lam_p, subln, batch, t, n_keys, p0, lam_init, n_heads,
                    *, tq, tk):
    hw = 2 * _DIFF_HD
    tq = _tile(t, tq)
    tk = _tile(n_keys, tk)
    nq, nk = t // tq, n_keys // tk
    assert tq % _CHUNK == 0 and p0 % _CHUNK == 0

    def k_map(b, hh, qi, ki, off):
        need = (p0 + (qi + 1) * tq + tk - 1) // tk
        return (b * nk + jnp.minimum(ki, need - 1), off + hh)

    kern = functools.partial(_diff_attn_kernel, p0=p0, tq=tq, tk=tk, lam_init=lam_init)
    return pl.pallas_call(
        kern,
        grid=(batch, n_heads, nq, nk),
        in_specs=[
            pl.BlockSpec((tq, hw), lambda b, hh, qi, ki: (b * nq + qi, q_col + hh)),
            pl.BlockSpec((tk, hw), lambda b, hh, qi, ki: k_map(b, hh, qi, ki, k_col)),
            pl.BlockSpec((tk, hw), lambda b, hh, qi, ki: k_map(b, hh, qi, ki, v_col)),
            pl.BlockSpec(lam_p.shape, lambda b, hh, qi, ki: (0, 0)),
            pl.BlockSpec((1, hw), lambda b, hh, qi, ki: (0, 0)),
        ],
        out_specs=pl.BlockSpec((tq, hw), lambda b, hh, qi, ki: (b * nq + qi, hh)),
        out_shape=jax.ShapeDtypeStruct((batch * t, n_heads * hw), _BF16),
        scratch_shapes=[
            pltpu.VMEM((2, tq, hw), _BF16),
            pltpu.VMEM((2, tq, 1), _F32),
            pltpu.VMEM((2, tq, 1), _F32),
            pltpu.VMEM((2, tq, hw), _F32),
        ],
        compiler_params=_params("parallel", "parallel", "parallel", "arbitrary"),
        name="diff_attention",
    )(q_arr, k_arr, v_arr, lam_p, subln.reshape(1, hw))


def _router_kernel(x_ref, g_ref, wr_ref, idx_ref, wgt_ref):
    xn = _rms(x_ref[...]) * g_ref[...]
    logits = lax.dot_general(wr_ref[...], xn, _NT, preferred_element_type=_F32, precision=lax.Precision.HIGHEST)
    ne = logits.shape[0]
    row = lax.broadcasted_iota(jnp.int32, logits.shape, 0)
    v1 = jnp.max(logits, axis=0, keepdims=True)
    i1 = jnp.min(jnp.where(logits == v1, row, ne), axis=0, keepdims=True)
    rest = jnp.where(row == i1, -jnp.inf, logits)
    v2 = jnp.max(rest, axis=0, keepdims=True)
    i2 = jnp.min(jnp.where(rest == v2, row, ne), axis=0, keepdims=True)
    e2 = jnp.exp(v2 - v1)
    w1 = 1.0 / (1.0 + e2)
    idx_ref[...] = jnp.concatenate([i1, i2], axis=0)
    wgt_ref[...] = jnp.concatenate([w1, e2 * w1], axis=0)


def _router(x, gain, w_router_t, *, tm=512):
    n, d = x.shape
    ne = w_router_t.shape[0]
    tm = _tile(n, tm)
    return pl.pallas_call(
        _router_kernel,
        grid=(n // tm,),
        in_specs=[
            pl.BlockSpec((tm, d), lambda i: (i, 0)),
            pl.BlockSpec((1, d), lambda i: (0, 0)),
            pl.BlockSpec((ne, d), lambda i: (0, 0)),
        ],
        out_specs=[
            pl.BlockSpec((_TOP_K, tm), lambda i: (0, i)),
            pl.BlockSpec((_TOP_K, tm), lambda i: (0, i)),
        ],
        out_shape=[
            jax.ShapeDtypeStruct((_TOP_K, n), jnp.int32),
            jax.ShapeDtypeStruct((_TOP_K, n), _F32),
        ],
        compiler_params=_params("parallel"),
        name="moe_router",
    )(x, gain.reshape(1, d), w_router_t)


def _gather_rows(src_hbm, dst_vmem, idx_at, n_rows, sem):
    def row_copy(r, src_row):
        return pltpu.make_async_copy(src_hbm.at[pl.ds(src_row, 1), :], dst_vmem.at[pl.ds(r, 1), :], sem)

    def start(r, c):
        row_copy(r, idx_at(r)).start()
        return c

    def wait(r, c):
        row_copy(r, 0).wait()
        return c

    lax.fori_loop(0, n_rows, start, 0)
    lax.fori_loop(0, n_rows, wait, 0)


def _expert_kernel(be_ref, nu_ref, rt_ref, x_hbm, g_ref, wg_ref, wu_ref, wd_ref, o_ref, xg_ref, xn_ref, acc_ref,
                   sem):
    del be_ref
    i = pl.program_id(0)
    j = pl.program_id(1)
    active = i < nu_ref[0]
    bm = xg_ref.shape[0]

    @pl.when(active & (j == 0))
    def _():
        _gather_rows(x_hbm, xg_ref, lambda r: rt_ref[0, 0, r], bm, sem)
        xn_ref[...] = (_rms(xg_ref[...]) * g_ref[...]).astype(_BF16)
        acc_ref[...] = jnp.zeros(acc_ref.shape, _F32)

    @pl.when(active)
    def _():
        xn = xn_ref[...]
        gate = jnp.dot(xn, wg_ref[0], preferred_element_type=_F32)
        up = jnp.dot(xn, wu_ref[0], preferred_element_type=_F32)
        act = (_silu(gate) * up).astype(_BF16)
        acc_ref[...] += jnp.dot(act, wd_ref[0], preferred_element_type=_F32)

    last = j == pl.num_programs(1) - 1

    @pl.when(active & last)
    def _():
        o_ref[...] = acc_ref[...]

    @pl.when(jnp.logical_not(active) & last)
    def _():
        o_ref[...] = jnp.zeros(o_ref.shape, _F32)


def _experts(x, gain, w_gu, w_down, row_tok, block_e, n_used, *, bm, tf=512):
    n, d = x.shape
    f = w_down.shape[1]
    tf = _tile(f, tf)
    nf = f // tf
    n_blocks = block_e.shape[0]

    def col(i, j, nu):
        return jnp.where(i < nu[0], j, nf - 1)

    grid_spec = pltpu.PrefetchScalarGridSpec(
        num_scalar_prefetch=2,
        grid=(n_blocks, nf),
        in_specs=[
            pl.BlockSpec((1, 1, bm), lambda i, j, be, nu: (i, 0, 0), memory_space=pltpu.SMEM),
            pl.BlockSpec(memory_space=pl.ANY),
            pl.BlockSpec((1, d), lambda i, j, be, nu: (0, 0)),
            pl.BlockSpec((1, d, tf), lambda i, j, be, nu: (be[i], 0, col(i, j, nu))),
            pl.BlockSpec((1, d, tf), lambda i, j, be, nu: (be[i], 0, nf + col(i, j, nu))),
            pl.BlockSpec((1, tf, d), lambda i, j, be, nu: (be[i], col(i, j, nu), 0)),
        ],
        out_specs=pl.BlockSpec((bm, d), lambda i, j, be, nu: (i, 0)),
        scratch_shapes=[
            pltpu.VMEM((bm, d), _F32),
            pltpu.VMEM((bm, d), _BF16),
            pltpu.VMEM((bm, d), _F32),
            pltpu.SemaphoreType.DMA(()),
        ],
    )
    return pl.pallas_call(
        _expert_kernel,
        grid_spec=grid_spec,
        out_shape=jax.ShapeDtypeStruct((n_blocks * bm, d), _F32),
        compiler_params=_params("arbitrary", "arbitrary"),
        name="moe_experts",
    )(block_e, n_used, row_tok.reshape(n_blocks, 1, bm), x, gain.reshape(1, d), w_gu, w_gu, w_down)


def _combine_kernel(d_ref, x_ref, w_ref, y_hbm, gf_ref, o_ref, r0_ref, r1_ref, sem):
    tm = x_ref.shape[0]
    _gather_rows(y_hbm, r0_ref, lambda r: d_ref[0, 0, r], tm, sem)
    _gather_rows(y_hbm, r1_ref, lambda r: d_ref[0, 1, r], tm, sem)
    w = w_ref[...]
    x = x_ref[...] + w[:, 0:1] * r0_ref[...] + w[:, 1:2] * r1_ref[...]
    o_ref[...] = _rms(x) * gf_ref[...]


def _combine_final(x, dest, wgt, yb, gain_final, *, tm=256):
    n, d = x.shape
    tm = _tile(n, tm)
    nt = n // tm
    dest_b = dest.reshape(nt, tm, _TOP_K).transpose(0, 2, 1)
    return pl.pallas_call(
        _combine_kernel,
        grid=(nt,),
        in_specs=[
            pl.BlockSpec((1, _TOP_K, tm), lambda i: (i, 0, 0), memory_space=pltpu.SMEM),
            pl.BlockSpec((tm, d), lambda i: (i, 0)),
            pl.BlockSpec((tm, _TOP_K), lambda i: (i, 0)),
            pl.BlockSpec(memory_space=pl.ANY),
            pl.BlockSpec((1, d), lambda i: (0, 0)),
        ],
        out_specs=pl.BlockSpec((tm, d), lambda i: (i, 0)),
        out_shape=jax.ShapeDtypeStruct((n, d), _F32),
        scratch_shapes=[
            pltpu.VMEM((tm, d), _F32),
            pltpu.VMEM((tm, d), _F32),
            pltpu.SemaphoreType.DMA(()),
        ],
        compiler_params=_params("arbitrary"),
        name="moe_combine_final_norm",
    )(dest_b, x, wgt, yb, gain_final.reshape(1, d))


def _moe_final(x, gain, w_router, w_gu, w_down, gain_final, *, bm):
    n, d = x.shape
    ne = w_router.shape[1]
    idx, wgt = _router(x, gain, w_router.T)
    a = n * _TOP_K
    e_flat = idx.T.reshape(a)
    onehot = (e_flat[:, None] == jnp.arange(ne, dtype=jnp.int32)[None, :]).astype(jnp.int32)
    csum = jnp.cumsum(onehot, axis=0)
    rank = jnp.sum(csum * onehot, axis=1) - 1
    counts = csum[-1]
    padded = (counts + bm - 1) // bm * bm
    pad_end = jnp.cumsum(padded)
    pad_start = pad_end - padded
    dest = pad_start[e_flat] + rank
    n_blocks = -(-a // bm) + ne
    row_tok = jnp.zeros((n_blocks * bm,), jnp.int32).at[dest].set(jnp.arange(a, dtype=jnp.int32) // _TOP_K)
    block_start = jnp.arange(n_blocks, dtype=jnp.int32) * bm
    block_e = jnp.minimum(jnp.searchsorted(pad_end, block_start, side="right"), ne - 1).astype(jnp.int32)
    n_used = (pad_end[-1:] // bm).astype(jnp.int32)
    yb = _experts(x, gain, w_gu, w_down, row_tok, block_e, n_used, bm=bm)
    return _combine_final(x, dest.reshape(n, _TOP_K), wgt.T, yb, gain_final)


def _rope_tables(pos, dim, lanes_signed):
    half = dim // 2
    inv = _ROPE_THETA ** (-jnp.arange(half, dtype=_F32) / half)
    ang = pos.astype(_F32)[:, None] * inv[None, :]
    cos, sin = jnp.cos(ang), jnp.sin(ang)
    if not lanes_signed:
        return cos, sin
    reps = _LANES // dim
    cos_l = jnp.tile(jnp.concatenate([cos, cos], axis=1), (1, reps))
    sin_l = jnp.tile(jnp.concatenate([-sin, sin], axis=1), (1, reps))
    return cos_l, sin_l


def _trunk(x, pos, p0, ret_s0, past_k, past_v, mem_k, mem_v, w, *, big):
    b, t, d = x.shape
    n = b * t
    xf = x.reshape(n, d)
    tm = 512 if big else 256
    dk = d // _RET_HEADS
    hd_all = d

    cos, sin = _rope_tables(pos, dk, False)
    if cos.shape[0] % tm != 0:
        cos, sin = jnp.tile(cos, (b, 1)), jnp.tile(sin, (b, 1))
    qkvg = _proj(xf, w["norm_mix"][0], w["ret_w_in"], cos, sin, out_dtype=_BF16, half=dk // 2,
                 n_rope_cols=2 * d, n_first_cols=d, scale_first=1.0, scale_rest=dk ** -0.5, tm=tm)
    o, ret_state = _retention(qkvg, ret_s0, b, t)
    xf = _out_proj(o, w["ret_w_out"], xf, tm=tm)
    xf = _cross_attention(xf, w["norm_cross"][0], w["cross_wq"][0], w["cross_wo"][0],
                          mem_k[0].reshape(b, -1, d), mem_v[0].reshape(b, -1, d), b, t, tm=tm)
    xf = _dense_ffn(xf, w["norm_ffn"][0], w["ffn_w_gu"], w["ffn_w_down"], tm=tm)

    lam_init = 0.8 - 0.6 * math.exp(-0.3 * 1)
    cos, sin = _rope_tables(pos, _DIFF_HD, True)
    if cos.shape[0] % tm != 0:
        cos, sin = jnp.tile(cos, (b, 1)), jnp.tile(sin, (b, 1))
    qkv = _proj(xf, w["norm_mix"][1], w["diff_w_in"], cos, sin, out_dtype=_F32, half=_DIFF_HD // 2,
                n_rope_cols=2 * hd_all, n_first_cols=hd_all, scale_first=_DIFF_HD ** -0.5, scale_rest=1.0, tm=tm)
    n_heads = d // (2 * _DIFF_HD)
    k_rows = qkv[:, hd_all:2 * hd_all]
    v_rows = qkv[:, 2 * hd_all:]
    if past_k is None:
        o = _diff_attention(qkv, qkv, qkv, 0, n_heads, 2 * n_heads, w["diff_lambda"], w["diff_subln"], b, t, t, p0,
                            lam_init, n_heads, tq=512, tk=512)
    else:
        n_past = past_k.shape[1]
        k_all = jnp.concatenate([past_k.reshape(b, n_past, d), k_rows.reshape(b, t, d)], axis=1)
        v_all = jnp.concatenate([past_v.reshape(b, n_past, d), v_rows.reshape(b, t, d)], axis=1)
        n_keys = n_past + t
        o = _diff_attention(qkv, k_all.reshape(b * n_keys, d), v_all.reshape(b * n_keys, d), 0, 0, 0,
                            w["diff_lambda"], w["diff_subln"], b, t, n_keys, p0, lam_init, n_heads, tq=t, tk=n_keys)
    xf = _out_proj(o, w["diff_w_out"], xf, tm=tm)
    xf = _cross_attention(xf, w["norm_cross"][1], w["cross_wq"][1], w["cross_wo"][1],
                          mem_k[1].reshape(b, -1, d), mem_v[1].reshape(b, -1, d), b, t, tm=tm)
    y = _moe_final(xf, w["norm_ffn"][1], w["moe_router"], w["moe_w_gu"], w["moe_w_down"], w["norm_final"],
                   bm=512 if big else 128)
    hw = 2 * _DIFF_HD
    return (y.reshape(b, t, d), ret_state, k_rows.reshape(b, t, n_heads, hw), v_rows.reshape(b, t, n_heads, hw))


def kernel(x_prompt, x_sample, state_ret, cache_diff_k, cache_diff_v, cache_mem_k, cache_mem_v, mem_prompt, norm_mix,
           norm_cross, norm_mem, norm_ffn, norm_final, ret_w_in, ret_w_out, diff_w_in, diff_lambda, diff_subln,
           diff_w_out, cross_wq, cross_wkv, cross_wo, ffn_w_gu, ffn_w_down, moe_router, moe_w_gu, moe_w_down):
    depth = norm_mix.shape[0]
    assert depth == 2, "one retention layer followed by one differential-attention layer"
    bf = lambda a: a.astype(_BF16)
    w = dict(norm_mix=norm_mix, norm_cross=norm_cross, norm_ffn=norm_ffn, norm_final=norm_final,
             ret_w_in=bf(ret_w_in[0]), ret_w_out=bf(ret_w_out[0]), diff_w_in=bf(diff_w_in[0]),
             diff_lambda=diff_lambda[0], diff_subln=diff_subln[0], diff_w_out=bf(diff_w_out[0]),
             cross_wq=bf(cross_wq), cross_wo=bf(cross_wo), ffn_w_gu=bf(ffn_w_gu[0]), ffn_w_down=bf(ffn_w_down[0]),
             moe_router=moe_router[0], moe_w_gu=bf(moe_w_gu[0]), moe_w_down=bf(moe_w_down[0]))

    b_p, t_p, d = x_prompt.shape
    b_s, t_s, _ = x_sample.shape
    mem_len = mem_prompt.shape[1]
    mem_flat = mem_prompt.reshape(b_p * mem_len, d)
    mks, mvs = [], []
    for i in range(depth):
        kv = _proj(mem_flat, norm_mem[i], bf(cross_wkv[i]), None, None, out_dtype=_F32, tm=256)
        mks.append(kv[:, :d].reshape(b_p, mem_len, _CROSS_HEADS, d // _CROSS_HEADS))
        mvs.append(kv[:, d:].reshape(b_p, mem_len, _CROSS_HEADS, d // _CROSS_HEADS))
    mem_k_p, mem_v_p = jnp.stack(mks), jnp.stack(mvs)

    dkr = d // _RET_HEADS
    s0 = jnp.zeros((b_p, _RET_HEADS, dkr, 2 * dkr), _F32)
    y_p, rs_p, dk_p, dv_p = _trunk(x_prompt, jnp.arange(t_p, dtype=jnp.int32), 0, s0, None, None, mem_k_p, mem_v_p, w,
                                   big=True)
    n_past = cache_diff_k.shape[2]
    y_s, rs_s, dk_s, dv_s = _trunk(x_sample, n_past + jnp.arange(t_s, dtype=jnp.int32), n_past, state_ret[0],
                                   cache_diff_k[0], cache_diff_v[0], cache_mem_k, cache_mem_v, w, big=False)
    return (y_p, y_s, rs_p[None], dk_p[None], dv_p[None], mem_k_p, mem_v_p, rs_s[None], dk_s[None], dv_s[None])
```

```python
import functools
import math

import jax
import jax.numpy as jnp
from jax import lax
from jax.experimental import pallas as pl
from jax.experimental.pallas import tpu as pltpu

_CHUNK = 64
_ROPE_THETA = 10000.0
_EPS = 1e-6
_RET_HEADS = 4
_DIFF_HD = 64
_CROSS_HEADS = 4
_N_EXPERTS = 8
_TOP_K = 2

_VMEM_LIMIT_V7X = 56 * 1024 * 1024
_LANES = 128

_F32 = jnp.float32
_BF16 = jnp.bfloat16
_NT = (((1,), (1,)), ((), ()))
_TN = (((0,), (0,)), ((), ()))
_NEG = -1e30


def _params(*sem):
    return pltpu.CompilerParams(dimension_semantics=sem, vmem_limit_bytes=_VMEM_LIMIT_V7X)


def _tile(n, pref):
    t = min(n, pref)
    assert n % t == 0, (n, pref)
    return t


def _rms(x):
    return x * lax.rsqrt(jnp.mean(x * x, axis=-1, keepdims=True) + _EPS)


def _silu(x):
    return x * jax.nn.sigmoid(x)


def _proj_kernel(x_ref, g_ref, w_ref, cos_ref, sin_ref, *rest, half, n_rope, n_first, scale_first, scale_rest,
                 out_ranges):
    o_refs, xn_ref = rest[:-1], rest[-1]
    j = pl.program_id(1)

    @pl.when(j == 0)
    def _():
        xn_ref[...] = (_rms(x_ref[...]) * g_ref[...]).astype(_BF16)

    acc = jnp.dot(xn_ref[...], w_ref[...], preferred_element_type=_F32)
    tn = acc.shape[1]

    def emit(val, lo, hi):
        for o_ref, (olo, ohi) in zip(o_refs, out_ranges):
            if olo >= hi or ohi <= lo:
                continue
            if olo <= lo and hi <= ohi:
                o_ref[...] = val.astype(o_ref.dtype)
            else:
                @pl.when((j >= olo) & (j < ohi))
                def _(o_ref=o_ref):
                    o_ref[...] = val.astype(o_ref.dtype)

    if n_rope > 0:
        @pl.when(j < n_rope)
        def _():
            cos = cos_ref[...]
            sin = sin_ref[...]
            pieces = []
            if half == _LANES:
                for c in range(tn // (2 * _LANES)):
                    x1 = acc[:, 2 * c * _LANES:(2 * c + 1) * _LANES]
                    x2 = acc[:, (2 * c + 1) * _LANES:(2 * c + 2) * _LANES]
                    pieces += [x1 * cos - x2 * sin, x2 * cos + x1 * sin]
            else:
                lane = lax.broadcasted_iota(jnp.int32, (1, _LANES), 1)
                first = (lane % (2 * half)) < half
                for c in range(tn // _LANES):
                    xc = acc[:, c * _LANES:(c + 1) * _LANES]
                    rot = jnp.where(first, pltpu.roll(xc, _LANES - half, 1), pltpu.roll(xc, half, 1))
                    pieces.append(xc * cos + rot * sin)
            r = jnp.concatenate(pieces, axis=1)
            scale = jnp.where(j < n_first, scale_first, scale_rest).astype(_F32)
            emit(r * scale, 0, n_rope)

        @pl.when(j >= n_rope)
        def _():
            emit(acc, n_rope, 1 << 30)
    else:
        emit(acc, 0, 1 << 30)


def _proj(x, gain, w, cos, sin, *, outs, half=_LANES, n_rope_cols=0, n_first_cols=0, scale_first=1.0,
          scale_rest=1.0, tm=512, tn=512):
    n, d = x.shape
    d_out = w.shape[1]
    tm = _tile(n, tm)
    tn = _tile(d_out, tn)
    assert n_rope_cols % tn == 0 and n_first_cols % tn == 0
    if cos is None:
        cos = jnp.zeros((tm, _LANES), _F32)
        sin = cos
    r_tiles = cos.shape[0] // tm
    assert cos.shape[0] % tm == 0
    out_ranges = []
    for _, lo, hi in outs:
        assert lo % tn == 0 and hi % tn == 0
        out_ranges.append((lo // tn, hi // tn))
    kern = functools.partial(_proj_kernel, half=half, n_rope=n_rope_cols // tn, n_first=n_first_cols // tn,
                             scale_first=scale_first, scale_rest=scale_rest, out_ranges=tuple(out_ranges))

    def out_map(lo, hi):
        return lambda i, j: (i, jnp.clip(j - lo, 0, hi - lo - 1))

    return pl.pallas_call(
        kern,
        grid=(n // tm, d_out // tn),
        in_specs=[
            pl.BlockSpec((tm, d), lambda i, j: (i, 0)),
            pl.BlockSpec((1, d), lambda i, j: (0, 0)),
            pl.BlockSpec((d, tn), lambda i, j: (0, j)),
            pl.BlockSpec((tm, _LANES), lambda i, j: (i % r_tiles, 0)),
            pl.BlockSpec((tm, _LANES), lambda i, j: (i % r_tiles, 0)),
        ],
        out_specs=[pl.BlockSpec((tm, tn), out_map(lo, hi)) for lo, hi in out_ranges],
        out_shape=[jax.ShapeDtypeStruct((n, hi - lo), dt) for dt, lo, hi in outs],
        scratch_shapes=[pltpu.VMEM((tm, d), _BF16)],
        compiler_params=_params("parallel", "arbitrary"),
        name="norm_proj",
    )(x, gain.reshape(1, d), w, cos, sin)


def _out_proj_kernel(a_ref, w_ref, r_ref, o_ref):
    o_ref[...] = r_ref[...] + jnp.dot(a_ref[...], w_ref[...], preferred_element_type=_F32)


def _out_proj(a, w, res, *, tm=512):
    n, k = a.shape
    d = w.shape[1]
    tm = _tile(n, tm)
    return pl.pallas_call(
        _out_proj_kernel,
        grid=(n // tm,),
        in_specs=[
            pl.BlockSpec((tm, k), lambda i: (i, 0)),
            pl.BlockSpec((k, d), lambda i: (0, 0)),
            pl.BlockSpec((tm, d), lambda i: (i, 0)),
        ],
        out_specs=pl.BlockSpec((tm, d), lambda i: (i, 0)),
        out_shape=jax.ShapeDtypeStruct((n, d), _F32),
        compiler_params=_params("parallel"),
        name="out_proj_residual",
    )(a, w, res)


def _retention_kernel(q_ref, k_ref, v_ref, g_ref, s0_ref, dm_ref, qd_ref, kd_ref, cd_ref, o_ref, so_ref, s_ref):
    n = pl.program_id(2)

    @pl.when(n == 0)
    def _():
        s_ref[...] = s0_ref[0, 0]

    q = q_ref[...]
    k = k_ref[...]
    v = v_ref[...]
    sc = lax.dot_general(q, k, _NT, preferred_element_type=_F32)
    p = (sc * dm_ref[0]).astype(_BF16)
    s_old = s_ref[...]
    q_dec = (q.astype(_F32) * qd_ref[0]).astype(_BF16)
    o = jnp.dot(p, v, preferred_element_type=_F32) + jnp.dot(q_dec, s_old.astype(_BF16),
                                                            preferred_element_type=_F32)
    k_dec = (k.astype(_F32) * kd_ref[0]).astype(_BF16)
    s_ref[...] = s_old * cd_ref[0] + lax.dot_general(k_dec, v, _TN, preferred_element_type=_F32)
    g = g_ref[...].astype(_F32)
    o_ref[...] = (_silu(g) * _rms(o)).astype(o_ref.dtype)

    @pl.when(n == pl.num_programs(2) - 1)
    def _():
        so_ref[0, 0] = s_ref[...]


def _retention(qkvg, s0, batch, t, *, block=256):
    h = _RET_HEADS
    dk, dv = s0.shape[2], s0.shape[3]
    lb = _tile(t, block)
    nb = t // lb
    assert lb % _CHUNK == 0
    log_g = jnp.log1p(-(2.0 ** (-5.0 - jnp.arange(h, dtype=_F32))))
    i = jnp.arange(lb, dtype=_F32)
    ci = jnp.arange(lb) // _CHUNK
    diff = i[:, None] - i[None, :]
    same = ci[:, None] == ci[None, :]
    before = ci[None, :] < ci[:, None]
    expo = jnp.where(same, jnp.abs(diff), diff)
    dmask = jnp.where((same | before)[None], jnp.exp(expo[None] * log_g[:, None, None]), 0.0).astype(_F32)
    q_dec = jnp.exp((i[None, :] + 1.0) * log_g[:, None])[..., None]
    k_dec = jnp.exp((lb - 1.0 - i)[None, :] * log_g[:, None])[..., None]
    c_dec = jnp.exp(lb * log_g).reshape(h, 1, 1)
    kq, kv = dk, dv
    q_blocks = (h * dk) // kq
    v_off = (2 * h * dk) // kv
    g_off = v_off + h
    return pl.pallas_call(
        _retention_kernel,
        grid=(batch, h, nb),
        in_specs=[
            pl.BlockSpec((lb, kq), lambda b, hh, n: (b * nb + n, hh)),
            pl.BlockSpec((lb, kq), lambda b, hh, n: (b * nb + n, q_blocks + hh)),
            pl.BlockSpec((lb, kv), lambda b, hh, n: (b * nb + n, v_off + hh)),
            pl.BlockSpec((lb, kv), lambda b, hh, n: (b * nb + n, g_off + hh)),
            pl.BlockSpec((1, 1, dk, dv), lambda b, hh, n: (b, hh, 0, 0)),
            pl.BlockSpec((1, lb, lb), lambda b, hh, n: (hh, 0, 0)),
            pl.BlockSpec((1, lb, 1), lambda b, hh, n: (hh, 0, 0)),
            pl.BlockSpec((1, lb, 1), lambda b, hh, n: (hh, 0, 0)),
            pl.BlockSpec((1, 1, 1), lambda b, hh, n: (hh, 0, 0)),
        ],
        out_specs=[
            pl.BlockSpec((lb, kv), lambda b, hh, n: (b * nb + n, hh)),
            pl.BlockSpec((1, 1, dk, dv), lambda b, hh, n: (b, hh, 0, 0)),
        ],
        out_shape=[
            jax.ShapeDtypeStruct((batch * t, h * dv), _BF16),
            jax.ShapeDtypeStruct((batch, h, dk, dv), _F32),
        ],
        scratch_shapes=[pltpu.VMEM((dk, dv), _F32)],
        compiler_params=_params("parallel", "parallel", "arbitrary"),
        name="retention",
    )(qkvg, qkvg, qkvg, qkvg, s0, dmask, q_dec, k_dec, c_dec)


def _cross_kernel(x_ref, g_ref, wq_ref, wo_ref, mk_ref, mv_ref, o_ref, *, heads):
    x = x_ref[...]
    d = x.shape[1]
    hd = d // heads
    xn = (_rms(x) * g_ref[...]).astype(_BF16)
    q = (jnp.dot(xn, wq_ref[...], preferred_element_type=_F32) * (hd ** -0.5)).astype(_BF16)
    mk = mk_ref[0].astype(_BF16)
    mv = mv_ref[0].astype(_BF16)
    outs = []
    for hh in range(heads):
        sl = slice(hh * hd, (hh + 1) * hd)
        s = lax.dot_general(q[:, sl], mk[:, sl], _NT, preferred_element_type=_F32)
        s = s - jnp.max(s, axis=-1, keepdims=True)
        e = jnp.exp(s)
        p = (e / jnp.sum(e, axis=-1, keepdims=True)).astype(_BF16)
        outs.append(jnp.dot(p, mv[:, sl], preferred_element_type=_F32).astype(_BF16))
    o = jnp.concatenate(outs, axis=1)
    o_ref[...] = x + jnp.dot(o, wo_ref[...], preferred_element_type=_F32)


def _cross_attention(x, gain, wq, wo, mk, mv, batch, t, *, tm=512):
    n, d = x.shape
    tm = _tile(t, tm)
    nt = t // tm
    mem = mk.shape[1]
    return pl.pallas_call(
        functools.partial(_cross_kernel, heads=_CROSS_HEADS),
        grid=(batch, nt),
        in_specs=[
            pl.BlockSpec((tm, d), lambda b, i: (b * nt + i, 0)),
            pl.BlockSpec((1, d), lambda b, i: (0, 0)),
            pl.BlockSpec((d, d), lambda b, i: (0, 0)),
            pl.BlockSpec((d, d), lambda b, i: (0, 0)),
            pl.BlockSpec((1, mem, d), lambda b, i: (b, 0, 0)),
            pl.BlockSpec((1, mem, d), lambda b, i: (b, 0, 0)),
        ],
        out_specs=pl.BlockSpec((tm, d), lambda b, i: (b * nt + i, 0)),
        out_shape=jax.ShapeDtypeStruct((n, d), _F32),
        compiler_params=_params("parallel", "parallel"),
        name="cross_attention",
    )(x, gain.reshape(1, d), wq, wo, mk, mv)


def _ffn_kernel(x_ref, g_ref, wg_ref, wu_ref, wd_ref, o_ref, xn_ref, acc_ref):
    j = pl.program_id(1)

    @pl.when(j == 0)
    def _():
        xn_ref[...] = (_rms(x_ref[...]) * g_ref[...]).astype(_BF16)
        acc_ref[...] = x_ref[...]

    xn = xn_ref[...]
    gate = jnp.dot(xn, wg_ref[...], preferred_element_type=_F32)
    up = jnp.dot(xn, wu_ref[...], preferred_element_type=_F32)
    act = (_silu(gate) * up).astype(_BF16)
    acc_ref[...] += jnp.dot(act, wd_ref[...], preferred_element_type=_F32)

    @pl.when(j == pl.num_programs(1) - 1)
    def _():
        o_ref[...] = acc_ref[...]


def _dense_ffn(x, gain, w_gu, w_down, *, tm=512, tf=256):
    n, d = x.shape
    f = w_down.shape[0]
    tm = _tile(n, tm)
    tf = _tile(f, tf)
    nf = f // tf
    return pl.pallas_call(
        _ffn_kernel,
        grid=(n // tm, nf),
        in_specs=[
            pl.BlockSpec((tm, d), lambda i, j: (i, 0)),
            pl.BlockSpec((1, d), lambda i, j: (0, 0)),
            pl.BlockSpec((d, tf), lambda i, j: (0, j)),
            pl.BlockSpec((d, tf), lambda i, j: (0, nf + j)),
            pl.BlockSpec((tf, d), lambda i, j: (j, 0)),
        ],
        out_specs=pl.BlockSpec((tm, d), lambda i, j: (i, 0)),
        out_shape=jax.ShapeDtypeStruct((n, d), _F32),
        scratch_shapes=[pltpu.VMEM((tm, d), _BF16), pltpu.VMEM((tm, d), _F32)],
        compiler_params=_params("parallel", "arbitrary"),
        name="dense_swiglu",
    )(x, gain.reshape(1, d), w_gu, w_gu, w_down)


def _diff_attn_kernel(q_ref, k_ref, v_ref, lam_ref, sub_ref, o_ref, q2_ref, m_ref, acc_ref, s_ref, *, p0, tq, tk,
                      lam_init):
    qi = pl.program_id(2)
    hd = _DIFF_HD
    hw = 2 * hd

    q = q_ref[...]
    first = lax.broadcasted_iota(jnp.int32, (1, hw), 1) < hd
    q2_ref[0:tq, :] = jnp.where(first, q, 0.0).astype(_BF16)
    q2_ref[tq:2 * tq, :] = jnp.where(first, 0.0, q).astype(_BF16)
    m_ref[...] = jnp.full(m_ref.shape, _NEG, _F32)
    acc_ref[...] = jnp.zeros(acc_ref.shape, _F32)

    def scores(off, nk):
        return lax.dot_general(q2_ref[...], k_ref[pl.ds(off, nk), :], _NT, preferred_element_type=_F32)

    def softmax_pv(s, off, nk, visible):
        v_blk = v_ref[pl.ds(off, nk), :]
        v_ext = jnp.concatenate([v_blk, jnp.ones(v_blk.shape, v_blk.dtype)], axis=1)
        if visible is not None:
            s = jnp.where(visible, s, _NEG)
        m_old = m_ref[...]
        m_new = jnp.maximum(m_old, jnp.max(s, axis=-1, keepdims=True))
        alpha = jnp.exp2(m_old - m_new)
        if nk % _LANES == 0:
            p = jnp.exp2(s - jnp.concatenate([m_new] * (nk // _LANES), axis=1))
        else:
            p = jnp.exp2(s - m_new[:, 0:1])
        pv = jnp.dot(p.astype(_BF16), v_ext, preferred_element_type=_F32)
        acc_ref[...] = jnp.concatenate([alpha, alpha], axis=1) * acc_ref[...] + pv
        m_ref[...] = m_new

    q_start = p0 + qi * tq
    n_full = q_start // tk
    r_chunk = lax.broadcasted_iota(jnp.int32, (2 * tq, tq), 0) % tq // _CHUNK
    c_chunk = lax.broadcasted_iota(jnp.int32, (2 * tq, tq), 1) // _CHUNK
    vis = c_chunk <= r_chunk
    diag_off = pl.multiple_of(q_start, tq)

    if tq == tk:
        s_ref[0] = scores(0, tk)

        def body(j, carry):
            off0 = pl.multiple_of(2 * j * tk, tk)
            off1 = pl.multiple_of(off0 + tk, tk)
            s_ref[1] = scores(off1, tk)
            softmax_pv(s_ref[0], off0, tk, None)
            s_ref[0] = scores(pl.multiple_of(off1 + tk, tk), tk)
            softmax_pv(s_ref[1], off1, tk, None)
            return carry

        lax.fori_loop(0, n_full // 2, body, 0)

        @pl.when(n_full % 2 == 1)
        def _():
            s_ref[1] = scores(diag_off, tk)
            softmax_pv(s_ref[0], pl.multiple_of(diag_off - tk, tk), tk, None)
            softmax_pv(s_ref[1], diag_off, tq, vis)

        @pl.when(n_full % 2 == 0)
        def _():
            softmax_pv(s_ref[0], diag_off, tq, vis)
    else:
        def body(i, carry):
            off = pl.multiple_of(i * tk, tk)
            softmax_pv(scores(off, tk), off, tk, None)
            return carry

        lax.fori_loop(0, n_full, body, 0)
        softmax_pv(scores(diag_off, tq), diag_off, tq, vis)


    lp = lam_ref[...]
    lam = (jnp.exp(jnp.sum(lp[0:1] * lp[1:2], axis=-1, keepdims=True))
           - jnp.exp(jnp.sum(lp[2:3] * lp[3:4], axis=-1, keepdims=True)) + lam_init)
    acc = acc_ref[...]
    o = acc[0:tq, 0:hw] / acc[0:tq, hw:2 * hw] - lam * (acc[tq:2 * tq, 0:hw] / acc[tq:2 * tq, hw:2 * hw])
    o_ref[...] = (_rms(o) * sub_ref[...] * (1.0 - lam_init)).astype(o_ref.dtype)


def _diff_attention(q_arr, k_arr, v_arr, q_col, k_col, v_col, lam_p, subln, batch, t, p0, lam_init, n_heads, *, tq,
                    tk):
    hw = 2 * _DIFF_HD
    n_keys = p0 + t
    tq = _tile(t, tq)
    nq = t // tq
    tk = math.gcd(tk, p0, tq) if nq > 1 else math.gcd(tk, p0)
    assert tq % _CHUNK == 0 and tk % _CHUNK == 0
    kern = functools.partial(_diff_attn_kernel, p0=p0, tq=tq, tk=tk, lam_init=lam_init)
    return pl.pallas_call(
        kern,
        grid=(batch, n_heads, nq),
        in_specs=[
            pl.BlockSpec((tq, hw), lambda b, hh, qi: (b * nq + qi, q_col + hh)),
            pl.BlockSpec((n_keys, hw), lambda b, hh, qi: (b, k_col + hh)),
            pl.BlockSpec((n_keys, hw), lambda b, hh, qi: (b, v_col + hh)),
            pl.BlockSpec(lam_p.shape, lambda b, hh, qi: (0, 0)),
            pl.BlockSpec((1, hw), lambda b, hh, qi: (0, 0)),
        ],
        out_specs=pl.BlockSpec((tq, hw), lambda b, hh, qi: (b * nq + qi, hh)),
        out_shape=jax.ShapeDtypeStruct((batch * t, n_heads * hw), _BF16),
        scratch_shapes=[
            pltpu.VMEM((2 * tq, hw), _BF16),
            pltpu.VMEM((2 * tq, _LANES), _F32),
            pltpu.VMEM((2 * tq, 2 * hw), _F32),
            pltpu.VMEM((2, 2 * tq, tk) if tq == tk else (2, 8, _LANES), _F32),
        ],
        compiler_params=_params("parallel", "parallel", "arbitrary"),
        name="diff_attention",
    )(q_arr, k_arr, v_arr, lam_p, subln.reshape(1, hw))


def _router_kernel(x_ref, g_ref, wr_ref, idx_ref, wgt_ref):
    xn = _rms(x_ref[...]) * g_ref[...]
    logits = lax.dot_general(wr_ref[...], xn, _NT, preferred_element_type=_F32, precision=lax.Precision.HIGHEST)
    ne = logits.shape[0]
    row = lax.broadcasted_iota(jnp.int32, logits.shape, 0)
    v1 = jnp.max(logits, axis=0, keepdims=True)
    i1 = jnp.min(jnp.where(logits == v1, row, ne), axis=0, keepdims=True)
    rest = jnp.where(row == i1, -jnp.inf, logits)
    v2 = jnp.max(rest, axis=0, keepdims=True)
    i2 = jnp.min(jnp.where(rest == v2, row, ne), axis=0, keepdims=True)
    e2 = jnp.exp(v2 - v1)
    w1 = 1.0 / (1.0 + e2)
    idx_ref[...] = jnp.concatenate([i1, i2], axis=0)
    wgt_ref[...] = jnp.concatenate([w1, e2 * w1], axis=0)


def _router(x, gain, w_router_t, *, tm=512):
    n, d = x.shape
    ne = w_router_t.shape[0]
    tm = _tile(n, tm)
    return pl.pallas_call(
        _router_kernel,
        grid=(n // tm,),
        in_specs=[
            pl.BlockSpec((tm, d), lambda i: (i, 0)),
            pl.BlockSpec((1, d), lambda i: (0, 0)),
            pl.BlockSpec((ne, d), lambda i: (0, 0)),
        ],
        out_specs=[
            pl.BlockSpec((_TOP_K, tm), lambda i: (0, i)),
            pl.BlockSpec((_TOP_K, tm), lambda i: (0, i)),
        ],
        out_shape=[
            jax.ShapeDtypeStruct((_TOP_K, n), jnp.int32),
            jax.ShapeDtypeStruct((_TOP_K, n), _F32),
        ],
        compiler_params=_params("parallel"),
        name="moe_router",
    )(x, gain.reshape(1, d), w_router_t)


def _gather_rows(src_hbm, dst_vmem, idx_at, n_rows, sem):
    def row_copy(r, src_row):
        return pltpu.make_async_copy(src_hbm.at[pl.ds(src_row, 1), :], dst_vmem.at[pl.ds(r, 1), :], sem)

    def start(r, c):
        row_copy(r, idx_at(r)).start()
        return c

    def wait(r, c):
        row_copy(r, 0).wait()
        return c

    lax.fori_loop(0, n_rows, start, 0)
    lax.fori_loop(0, n_rows, wait, 0)


def _expert_kernel(be_ref, nu_ref, rt_ref, x_hbm, g_ref, wg_ref, wu_ref, wd_ref, o_ref, xg_ref, xn_ref, acc_ref,
                   sem):
    del be_ref
    i = pl.program_id(0)
    j = pl.program_id(1)
    active = i < nu_ref[0]
    bm = xg_ref.shape[0]

    @pl.when(active & (j == 0))
    def _():
        _gather_rows(x_hbm, xg_ref, lambda r: rt_ref[0, 0, r], bm, sem)
        xn_ref[...] = (_rms(xg_ref[...]) * g_ref[...]).astype(_BF16)
        acc_ref[...] = jnp.zeros(acc_ref.shape, _F32)

    @pl.when(active)
    def _():
        xn = xn_ref[...]
        gate = jnp.dot(xn, wg_ref[0], preferred_element_type=_F32)
        up = jnp.dot(xn, wu_ref[0], preferred_element_type=_F32)
        act = (_silu(gate) * up).astype(_BF16)
        acc_ref[...] += jnp.dot(act, wd_ref[0], preferred_element_type=_F32)

    last = j == pl.num_programs(1) - 1

    @pl.when(active & last)
    def _():
        o_ref[...] = acc_ref[...]

    @pl.when(jnp.logical_not(active) & last)
    def _():
        o_ref[...] = jnp.zeros(o_ref.shape, _F32)


def _experts(x, gain, w_gu, w_down, row_tok, block_e, n_used, *, bm, tf=512):
    n, d = x.shape
    f = w_down.shape[1]
    tf = _tile(f, tf)
    nf = f // tf
    n_blocks = block_e.shape[0]

    def col(i, j, nu):
        return jnp.where(i < nu[0], j, nf - 1)

    grid_spec = pltpu.PrefetchScalarGridSpec(
        num_scalar_prefetch=2,
        grid=(n_blocks, nf),
        in_specs=[
            pl.BlockSpec((1, 1, bm), lambda i, j, be, nu: (i, 0, 0), memory_space=pltpu.SMEM),
            pl.BlockSpec(memory_space=pl.ANY),
            pl.BlockSpec((1, d), lambda i, j, be, nu: (0, 0)),
            pl.BlockSpec((1, d, tf), lambda i, j, be, nu: (be[i], 0, col(i, j, nu))),
            pl.BlockSpec((1, d, tf), lambda i, j, be, nu: (be[i], 0, nf + col(i, j, nu))),
            pl.BlockSpec((1, tf, d), lambda i, j, be, nu: (be[i], col(i, j, nu), 0)),
        ],
        out_specs=pl.BlockSpec((bm, d), lambda i, j, be, nu: (i, 0)),
        scratch_shapes=[
            pltpu.VMEM((bm, d), _F32),
            pltpu.VMEM((bm, d), _BF16),
            pltpu.VMEM((bm, d), _F32),
            pltpu.SemaphoreType.DMA(()),
        ],
    )
    return pl.pallas_call(
        _expert_kernel,
        grid_spec=grid_spec,
        out_shape=jax.ShapeDtypeStruct((n_blocks * bm, d), _F32),
        compiler_params=_params("arbitrary", "arbitrary"),
        name="moe_experts",
    )(block_e, n_used, row_tok.reshape(n_blocks, 1, bm), x, gain.reshape(1, d), w_gu, w_gu, w_down)


def _combine_kernel(d_ref, x_ref, w_ref, y_hbm, gf_ref, o_ref, r0_ref, r1_ref, sem):
    tm = x_ref.shape[0]
    _gather_rows(y_hbm, r0_ref, lambda r: d_ref[0, 0, r], tm, sem)
    _gather_rows(y_hbm, r1_ref, lambda r: d_ref[0, 1, r], tm, sem)
    w = w_ref[...]
    x = x_ref[...] + w[:, 0:1] * r0_ref[...] + w[:, 1:2] * r1_ref[...]
    o_ref[...] = _rms(x) * gf_ref[...]


def _combine_final(x, dest, wgt, yb, gain_final, *, tm=256):
    n, d = x.shape
    tm = _tile(n, tm)
    nt = n // tm
    dest_b = dest.reshape(nt, tm, _TOP_K).transpose(0, 2, 1)
    return pl.pallas_call(
        _combine_kernel,
        grid=(nt,),
        in_specs=[
            pl.BlockSpec((1, _TOP_K, tm), lambda i: (i, 0, 0), memory_space=pltpu.SMEM),
            pl.BlockSpec((tm, d), lambda i: (i, 0)),
            pl.BlockSpec((tm, _TOP_K), lambda i: (i, 0)),
            pl.BlockSpec(memory_space=pl.ANY),
            pl.BlockSpec((1, d), lambda i: (0, 0)),
        ],
        out_specs=pl.BlockSpec((tm, d), lambda i: (i, 0)),
        out_shape=jax.ShapeDtypeStruct((n, d), _F32),
        scratch_shapes=[
            pltpu.VMEM((tm, d), _F32),
            pltpu.VMEM((tm, d), _F32),
            pltpu.SemaphoreType.DMA(()),
        ],
        compiler_params=_params("arbitrary"),
        name="moe_combine_final_norm",
    )(dest_b, x, wgt, yb, gain_final.reshape(1, d))


def _moe_final(x, gain, w_router, w_gu, w_down, gain_final, *, bm):
    n, d = x.shape
    ne = w_router.shape[1]
    idx, wgt = _router(x, gain, w_router.T)
    a = n * _TOP_K
    e_flat = idx.T.reshape(a)
    onehot = (e_flat[:, None] == jnp.arange(ne, dtype=jnp.int32)[None, :]).astype(jnp.int32)
    csum = jnp.cumsum(onehot, axis=0)
    rank = jnp.sum(csum * onehot, axis=1) - 1
    counts = csum[-1]
    padded = (counts + bm - 1) // bm * bm
    pad_end = jnp.cumsum(padded)
    pad_start = pad_end - padded
    dest = pad_start[e_flat] + rank
    n_blocks = -(-a // bm) + ne
    row_tok = jnp.zeros((n_blocks * bm,), jnp.int32).at[dest].set(jnp.arange(a, dtype=jnp.int32) // _TOP_K)
    block_start = jnp.arange(n_blocks, dtype=jnp.int32) * bm
    block_e = jnp.minimum(jnp.sum((block_start[:, None] >= pad_end[None, :]).astype(jnp.int32), axis=1), ne - 1)
    n_used = (pad_end[-1:] // bm).astype(jnp.int32)
    yb = _experts(x, gain, w_gu, w_down, row_tok, block_e, n_used, bm=bm)
    return _combine_final(x, dest.reshape(n, _TOP_K), wgt.T, yb, gain_final)


def _rope_tables(pos, dim, lanes_signed):
    half = dim // 2
    inv = _ROPE_THETA ** (-jnp.arange(half, dtype=_F32) / half)
    ang = pos.astype(_F32)[:, None] * inv[None, :]
    cos, sin = jnp.cos(ang), jnp.sin(ang)
    if not lanes_signed:
        return cos, sin
    reps = _LANES // dim
    cos_l = jnp.tile(jnp.concatenate([cos, cos], axis=1), (1, reps))
    sin_l = jnp.tile(jnp.concatenate([-sin, sin], axis=1), (1, reps))
    return cos_l, sin_l


def _trunk(x, pos, p0, ret_s0, past_k, past_v, mem_k, mem_v, w, *, big):
    b, t, d = x.shape
    n = b * t
    xf = x.reshape(n, d)
    tm = 512 if big else 256
    dk = d // _RET_HEADS
    hd_all = d

    cos, sin = _rope_tables(pos, dk, False)
    if cos.shape[0] % tm != 0:
        cos, sin = jnp.tile(cos, (b, 1)), jnp.tile(sin, (b, 1))
    (qkvg,) = _proj(xf, w["norm_mix"][0], w["ret_w_in"], cos, sin, outs=[(_BF16, 0, w["ret_w_in"].shape[1])],
                    half=dk // 2, n_rope_cols=2 * d, n_first_cols=d, scale_first=1.0, scale_rest=dk ** -0.5, tm=tm)
    o, ret_state = _retention(qkvg, ret_s0, b, t)
    xf = _out_proj(o, w["ret_w_out"], xf, tm=tm)
    xf = _cross_attention(xf, w["norm_cross"][0], w["cross_wq"][0], w["cross_wo"][0],
                          mem_k[0].reshape(b, -1, d), mem_v[0].reshape(b, -1, d), b, t, tm=tm)
    xf = _dense_ffn(xf, w["norm_ffn"][0], w["ffn_w_gu"], w["ffn_w_down"], tm=tm)

    lam_init = 0.8 - 0.6 * math.exp(-0.3 * 1)
    cos, sin = _rope_tables(pos, _DIFF_HD, True)
    if cos.shape[0] % tm != 0:
        cos, sin = jnp.tile(cos, (b, 1)), jnp.tile(sin, (b, 1))
    qkv, k_rows, v_rows = _proj(
        xf, w["norm_mix"][1], w["diff_w_in"], cos, sin,
        outs=[(_BF16, 0, 3 * hd_all), (_F32, hd_all, 2 * hd_all), (_F32, 2 * hd_all, 3 * hd_all)],
        half=_DIFF_HD // 2, n_rope_cols=2 * hd_all, n_first_cols=hd_all,
        scale_first=_DIFF_HD ** -0.5 * math.log2(math.e),
        scale_rest=1.0, tm=tm)
    n_heads = d // (2 * _DIFF_HD)
    if past_k is None:
        o = _diff_attention(qkv, qkv, qkv, 0, n_heads, 2 * n_heads, w["diff_lambda"], w["diff_subln"], b, t, p0,
                            lam_init, n_heads, tq=512, tk=512)
    else:
        n_past = past_k.shape[1]
        n_keys = n_past + t
        k_new = qkv[:, hd_all:2 * hd_all].reshape(b, t, d)
        v_new = qkv[:, 2 * hd_all:].reshape(b, t, d)
        k_all = jnp.concatenate([past_k.reshape(b, n_past, d).astype(_BF16), k_new], axis=1)
        v_all = jnp.concatenate([past_v.reshape(b, n_past, d).astype(_BF16), v_new], axis=1)
        o = _diff_attention(qkv, k_all.reshape(b * n_keys, d), v_all.reshape(b * n_keys, d), 0, 0, 0,
                            w["diff_lambda"], w["diff_subln"], b, t, p0, lam_init, n_heads, tq=t, tk=512)
    xf = _out_proj(o, w["diff_w_out"], xf, tm=tm)
    xf = _cross_attention(xf, w["norm_cross"][1], w["cross_wq"][1], w["cross_wo"][1],
                          mem_k[1].reshape(b, -1, d), mem_v[1].reshape(b, -1, d), b, t, tm=tm)
    y = _moe_final(xf, w["norm_ffn"][1], w["moe_router"], w["moe_w_gu"], w["moe_w_down"], w["norm_final"],
                   bm=512 if big else 128)
    hw = 2 * _DIFF_HD
    return (y.reshape(b, t, d), ret_state, k_rows.reshape(b, t, n_heads, hw), v_rows.reshape(b, t, n_heads, hw))


def kernel(x_prompt, x_sample, state_ret, cache_diff_k, cache_diff_v, cache_mem_k, cache_mem_v, mem_prompt, norm_mix,
           norm_cross, norm_mem, norm_ffn, norm_final, ret_w_in, ret_w_out, diff_w_in, diff_lambda, diff_subln,
           diff_w_out, cross_wq, cross_wkv, cross_wo, ffn_w_gu, ffn_w_down, moe_router, moe_w_gu, moe_w_down):
    depth = norm_mix.shape[0]
    assert depth == 2, "one retention layer followed by one differential-attention layer"
    bf = lambda a: a.astype(_BF16)
    w = dict(norm_mix=norm_mix, norm_cross=norm_cross, norm_ffn=norm_ffn, norm_final=norm_final,
             ret_w_in=bf(ret_w_in[0]), ret_w_out=bf(ret_w_out[0]), diff_w_in=bf(diff_w_in[0]),
             diff_lambda=diff_lambda[0], diff_subln=diff_subln[0], diff_w_out=bf(diff_w_out[0]),
             cross_wq=bf(cross_wq), cross_wo=bf(cross_wo), ffn_w_gu=bf(ffn_w_gu[0]), ffn_w_down=bf(ffn_w_down[0]),
             moe_router=moe_router[0], moe_w_gu=bf(moe_w_gu[0]), moe_w_down=bf(moe_w_down[0]))

    b_p, t_p, d = x_prompt.shape
    b_s, t_s, _ = x_sample.shape
    mem_len = mem_prompt.shape[1]
    mem_flat = mem_prompt.reshape(b_p * mem_len, d)
    mks, mvs = [], []
    for i in range(depth):
        mk, mv = _proj(mem_flat, norm_mem[i], bf(cross_wkv[i]), None, None, outs=[(_F32, 0, d), (_F32, d, 2 * d)],
                       tm=256)
        mks.append(mk.reshape(b_p, mem_len, _CROSS_HEADS, d // _CROSS_HEADS))
        mvs.append(mv.reshape(b_p, mem_len, _CROSS_HEADS, d // _CROSS_HEADS))
    mem_k_p, mem_v_p = jnp.stack(mks), jnp.stack(mvs)

    dkr = d // _RET_HEADS
    s0 = jnp.zeros((b_p, _RET_HEADS, dkr, 2 * dkr), _F32)
    y_p, rs_p, dk_p, dv_p = _trunk(x_prompt, jnp.arange(t_p, dtype=jnp.int32), 0, s0, None, None, mem_k_p, mem_v_p, w,
                                   big=True)
    n_past = cache_diff_k.shape[2]
    y_s, rs_s, dk_s, dv_s = _trunk(x_sample, n_past + jnp.arange(t_s, dtype=jnp.int32), n_past, state_ret[0],
                                   cache_diff_k[0], cache_diff_v[0], cache_mem_k, cache_mem_v, w, big=False)
    return (y_p, y_s, rs_p[None], dk_p[None], dv_p[None], mem_k_p, mem_v_p, rs_s[None], dk_s[None], dv_s[None])
```

```python
import functools
import math

import jax
import jax.numpy as jnp
from jax import lax
from jax.experimental import pallas as pl
from jax.experimental.pallas import tpu as pltpu

_CHUNK = 64
_ROPE_THETA = 10000.0
_EPS = 1e-6
_RET_HEADS = 4
_DIFF_HD = 64
_CROSS_HEADS = 4
_N_EXPERTS = 8
_TOP_K = 2

_VMEM_LIMIT_V7X = 56 * 1024 * 1024
_LANES = 128

_F32 = jnp.float32
_BF16 = jnp.bfloat16
_NT = (((1,), (1,)), ((), ()))
_TN = (((0,), (0,)), ((), ()))
_NEG = -1e30


def _params(*sem):
    return pltpu.CompilerParams(dimension_semantics=sem, vmem_limit_bytes=_VMEM_LIMIT_V7X)


def _tile(n, pref):
    t = min(n, pref)
    assert n % t == 0, (n, pref)
    return t


def _rms(x):
    return x * lax.rsqrt(jnp.mean(x * x, axis=-1, keepdims=True) + _EPS)


def _silu(x):
    return x * jax.nn.sigmoid(x)


def _proj_kernel(x_ref, g_ref, w_ref, cos_ref, sin_ref, *rest, half, n_rope, n_first, scale_first, scale_rest,
                 out_ranges):
    o_refs, xn_ref = rest[:-1], rest[-1]
    j = pl.program_id(1)

    @pl.when(j == 0)
    def _():
        xn_ref[...] = (_rms(x_ref[...]) * g_ref[...]).astype(_BF16)

    acc = jnp.dot(xn_ref[...], w_ref[...], preferred_element_type=_F32)
    tn = acc.shape[1]

    def emit(val, lo, hi):
        for o_ref, (olo, ohi) in zip(o_refs, out_ranges):
            if olo >= hi or ohi <= lo:
                continue
            if olo <= lo and hi <= ohi:
                o_ref[...] = val.astype(o_ref.dtype)
            else:
                @pl.when((j >= olo) & (j < ohi))
                def _(o_ref=o_ref):
                    o_ref[...] = val.astype(o_ref.dtype)

    if n_rope > 0:
        @pl.when(j < n_rope)
        def _():
            cos = cos_ref[...]
            sin = sin_ref[...]
            pieces = []
            if half == _LANES:
                for c in range(tn // (2 * _LANES)):
                    x1 = acc[:, 2 * c * _LANES:(2 * c + 1) * _LANES]
                    x2 = acc[:, (2 * c + 1) * _LANES:(2 * c + 2) * _LANES]
                    pieces += [x1 * cos - x2 * sin, x2 * cos + x1 * sin]
            else:
                lane = lax.broadcasted_iota(jnp.int32, (1, _LANES), 1)
                first = (lane % (2 * half)) < half
                for c in range(tn // _LANES):
                    xc = acc[:, c * _LANES:(c + 1) * _LANES]
                    rot = jnp.where(first, pltpu.roll(xc, _LANES - half, 1), pltpu.roll(xc, half, 1))
                    pieces.append(xc * cos + rot * sin)
            r = jnp.concatenate(pieces, axis=1)
            scale = jnp.where(j < n_first, scale_first, scale_rest).astype(_F32)
            emit(r * scale, 0, n_rope)

        @pl.when(j >= n_rope)
        def _():
            emit(acc, n_rope, 1 << 30)
    else:
        emit(acc, 0, 1 << 30)


def _proj(x, gain, w, cos, sin, *, outs, half=_LANES, n_rope_cols=0, n_first_cols=0, scale_first=1.0,
          scale_rest=1.0, tm=512, tn=512):
    n, d = x.shape
    d_out = w.shape[1]
    tm = _tile(n, tm)
    tn = _tile(d_out, tn)
    assert n_rope_cols % tn == 0 and n_first_cols % tn == 0
    if cos is None:
        cos = jnp.zeros((tm, _LANES), _F32)
        sin = cos
    r_tiles = cos.shape[0] // tm
    assert cos.shape[0] % tm == 0
    out_ranges = []
    for _, lo, hi in outs:
        assert lo % tn == 0 and hi % tn == 0
        out_ranges.append((lo // tn, hi // tn))
    kern = functools.partial(_proj_kernel, half=half, n_rope=n_rope_cols // tn, n_first=n_first_cols // tn,
                             scale_first=scale_first, scale_rest=scale_rest, out_ranges=tuple(out_ranges))

    def out_map(lo, hi):
        return lambda i, j: (i, jnp.clip(j - lo, 0, hi - lo - 1))

    return pl.pallas_call(
        kern,
        grid=(n // tm, d_out // tn),
        in_specs=[
            pl.BlockSpec((tm, d), lambda i, j: (i, 0)),
            pl.BlockSpec((1, d), lambda i, j: (0, 0)),
            pl.BlockSpec((d, tn), lambda i, j: (0, j)),
            pl.BlockSpec((tm, _LANES), lambda i, j: (i % r_tiles, 0)),
            pl.BlockSpec((tm, _LANES), lambda i, j: (i % r_tiles, 0)),
        ],
        out_specs=[pl.BlockSpec((tm, tn), out_map(lo, hi)) for lo, hi in out_ranges],
        out_shape=[jax.ShapeDtypeStruct((n, hi - lo), dt) for dt, lo, hi in outs],
        scratch_shapes=[pltpu.VMEM((tm, d), _BF16)],
        compiler_params=_params("parallel", "arbitrary"),
        name="norm_proj",
    )(x, gain.reshape(1, d), w, cos, sin)


def _out_proj_kernel(a_ref, w_ref, r_ref, o_ref):
    o_ref[...] = r_ref[...] + jnp.dot(a_ref[...], w_ref[...], preferred_element_type=_F32)


def _out_proj(a, w, res, *, tm=512):
    n, k = a.shape
    d = w.shape[1]
    tm = _tile(n, tm)
    return pl.pallas_call(
        _out_proj_kernel,
        grid=(n // tm,),
        in_specs=[
            pl.BlockSpec((tm, k), lambda i: (i, 0)),
            pl.BlockSpec((k, d), lambda i: (0, 0)),
            pl.BlockSpec((tm, d), lambda i: (i, 0)),
        ],
        out_specs=pl.BlockSpec((tm, d), lambda i: (i, 0)),
        out_shape=jax.ShapeDtypeStruct((n, d), _F32),
        compiler_params=_params("parallel"),
        name="out_proj_residual",
    )(a, w, res)


def _retention_kernel(q_ref, k_ref, v_ref, g_ref, s0_ref, dm_ref, qd_ref, kd_ref, cd_ref, o_ref, so_ref, s_ref):
    n = pl.program_id(1)
    heads, dk, dv = s_ref.shape

    @pl.when(n == 0)
    def _():
        s_ref[...] = s0_ref[0]

    for hh in range(heads):
        q = q_ref[:, hh * dk:(hh + 1) * dk]
        k = k_ref[:, hh * dk:(hh + 1) * dk]
        v = v_ref[:, hh * dv:(hh + 1) * dv]
        sc = lax.dot_general(q, k, _NT, preferred_element_type=_F32)
        p = (sc * dm_ref[hh]).astype(_BF16)
        s_old = s_ref[hh]
        q_dec = (q.astype(_F32) * qd_ref[hh]).astype(_BF16)
        o = jnp.dot(p, v, preferred_element_type=_F32) + jnp.dot(q_dec, s_old.astype(_BF16),
                                                                preferred_element_type=_F32)
        k_dec = (k.astype(_F32) * kd_ref[hh]).astype(_BF16)
        s_ref[hh] = s_old * cd_ref[hh] + lax.dot_general(k_dec, v, _TN, preferred_element_type=_F32)
        g = g_ref[:, hh * dv:(hh + 1) * dv].astype(_F32)
        o_ref[:, hh * dv:(hh + 1) * dv] = (_silu(g) * _rms(o)).astype(o_ref.dtype)

    @pl.when(n == pl.num_programs(1) - 1)
    def _():
        so_ref[0] = s_ref[...]


def _retention(qkvg, s0, batch, t, *, block=256):
    h = _RET_HEADS
    dk, dv = s0.shape[2], s0.shape[3]
    lb = _tile(t, block)
    nb = t // lb
    assert lb % _CHUNK == 0
    log_g = jnp.log1p(-(2.0 ** (-5.0 - jnp.arange(h, dtype=_F32))))
    i = jnp.arange(lb, dtype=_F32)
    ci = jnp.arange(lb) // _CHUNK
    diff = i[:, None] - i[None, :]
    same = ci[:, None] == ci[None, :]
    before = ci[None, :] < ci[:, None]
    expo = jnp.where(same, jnp.abs(diff), diff)
    dmask = jnp.where((same | before)[None], jnp.exp(expo[None] * log_g[:, None, None]), 0.0).astype(_F32)
    q_dec = jnp.exp((i[None, :] + 1.0) * log_g[:, None])[..., None]
    k_dec = jnp.exp((lb - 1.0 - i)[None, :] * log_g[:, None])[..., None]
    c_dec = jnp.exp(lb * log_g).reshape(h, 1, 1)
    wq, wv = h * dk, h * dv
    assert (2 * wq) % wv == 0
    v_blk = (2 * wq) // wv
    return pl.pallas_call(
        _retention_kernel,
        grid=(batch, nb),
        in_specs=[
            pl.BlockSpec((lb, wq), lambda b, n: (b * nb + n, 0)),
            pl.BlockSpec((lb, wq), lambda b, n: (b * nb + n, 1)),
            pl.BlockSpec((lb, wv), lambda b, n: (b * nb + n, v_blk)),
            pl.BlockSpec((lb, wv), lambda b, n: (b * nb + n, v_blk + 1)),
            pl.BlockSpec((1, h, dk, dv), lambda b, n: (b, 0, 0, 0)),
            pl.BlockSpec((h, lb, lb), lambda b, n: (0, 0, 0)),
            pl.BlockSpec((h, lb, 1), lambda b, n: (0, 0, 0)),
            pl.BlockSpec((h, lb, 1), lambda b, n: (0, 0, 0)),
            pl.BlockSpec((h, 1, 1), lambda b, n: (0, 0, 0)),
        ],
        out_specs=[
            pl.BlockSpec((lb, wv), lambda b, n: (b * nb + n, 0)),
            pl.BlockSpec((1, h, dk, dv), lambda b, n: (b, 0, 0, 0)),
        ],
        out_shape=[
            jax.ShapeDtypeStruct((batch * t, wv), _BF16),
            jax.ShapeDtypeStruct((batch, h, dk, dv), _F32),
        ],
        scratch_shapes=[pltpu.VMEM((h, dk, dv), _F32)],
        compiler_params=_params("parallel", "arbitrary"),
        name="retention",
    )(qkvg, qkvg, qkvg, qkvg, s0, dmask, q_dec, k_dec, c_dec)


def _cross_kernel(x_ref, g_ref, wq_ref, wo_ref, mk_ref, mv_ref, o_ref, *, heads):
    x = x_ref[...]
    d = x.shape[1]
    hd = d // heads
    xn = (_rms(x) * g_ref[...]).astype(_BF16)
    q = (jnp.dot(xn, wq_ref[...], preferred_element_type=_F32) * (hd ** -0.5)).astype(_BF16)
    mk = mk_ref[0].astype(_BF16)
    mv = mv_ref[0].astype(_BF16)
    outs = []
    for hh in range(heads):
        sl = slice(hh * hd, (hh + 1) * hd)
        s = lax.dot_general(q[:, sl], mk[:, sl], _NT, preferred_element_type=_F32)
        s = s - jnp.max(s, axis=-1, keepdims=True)
        e = jnp.exp(s)
        p = (e / jnp.sum(e, axis=-1, keepdims=True)).astype(_BF16)
        outs.append(jnp.dot(p, mv[:, sl], preferred_element_type=_F32).astype(_BF16))
    o = jnp.concatenate(outs, axis=1)
    o_ref[...] = x + jnp.dot(o, wo_ref[...], preferred_element_type=_F32)


def _cross_attention(x, gain, wq, wo, mk, mv, batch, t, *, tm=512):
    n, d = x.shape
    tm = _tile(t, tm)
    nt = t // tm
    mem = mk.shape[1]
    return pl.pallas_call(
        functools.partial(_cross_kernel, heads=_CROSS_HEADS),
        grid=(batch, nt),
        in_specs=[
            pl.BlockSpec((tm, d), lambda b, i: (b * nt + i, 0)),
            pl.BlockSpec((1, d), lambda b, i: (0, 0)),
            pl.BlockSpec((d, d), lambda b, i: (0, 0)),
            pl.BlockSpec((d, d), lambda b, i: (0, 0)),
            pl.BlockSpec((1, mem, d), lambda b, i: (b, 0, 0)),
            pl.BlockSpec((1, mem, d), lambda b, i: (b, 0, 0)),
        ],
        out_specs=pl.BlockSpec((tm, d), lambda b, i: (b * nt + i, 0)),
        out_shape=jax.ShapeDtypeStruct((n, d), _F32),
        compiler_params=_params("parallel", "parallel"),
        name="cross_attention",
    )(x, gain.reshape(1, d), wq, wo, mk, mv)


def _ffn_kernel(x_ref, g_ref, wg_ref, wu_ref, wd_ref, o_ref, xn_ref, acc_ref):
    j = pl.program_id(1)

    @pl.when(j == 0)
    def _():
        xn_ref[...] = (_rms(x_ref[...]) * g_ref[...]).astype(_BF16)
        acc_ref[...] = x_ref[...]

    xn = xn_ref[...]
    gate = jnp.dot(xn, wg_ref[...], preferred_element_type=_F32)
    up = jnp.dot(xn, wu_ref[...], preferred_element_type=_F32)
    act = (_silu(gate) * up).astype(_BF16)
    acc_ref[...] += jnp.dot(act, wd_ref[...], preferred_element_type=_F32)

    @pl.when(j == pl.num_programs(1) - 1)
    def _():
        o_ref[...] = acc_ref[...]


def _dense_ffn(x, gain, w_gu, w_down, *, tm=512, tf=1408):
    n, d = x.shape
    f = w_down.shape[0]
    tm = _tile(n, tm)
    tf = _tile(f, tf)
    nf = f // tf
    return pl.pallas_call(
        _ffn_kernel,
        grid=(n // tm, nf),
        in_specs=[
            pl.BlockSpec((tm, d), lambda i, j: (i, 0)),
            pl.BlockSpec((1, d), lambda i, j: (0, 0)),
            pl.BlockSpec((d, tf), lambda i, j: (0, j)),
            pl.BlockSpec((d, tf), lambda i, j: (0, nf + j)),
            pl.BlockSpec((tf, d), lambda i, j: (j, 0)),
        ],
        out_specs=pl.BlockSpec((tm, d), lambda i, j: (i, 0)),
        out_shape=jax.ShapeDtypeStruct((n, d), _F32),
        scratch_shapes=[pltpu.VMEM((tm, d), _BF16), pltpu.VMEM((tm, d), _F32)],
        compiler_params=_params("parallel", "arbitrary"),
        name="dense_swiglu",
    )(x, gain.reshape(1, d), w_gu, w_gu, w_down)


def _diff_attn_kernel(q_ref, k_ref, v_ref, lam_ref, sub_ref, o_ref, q2_ref, m_ref, acc_ref, s_ref, *, p0, tq, tk,
                      lam_init):
    qi = pl.program_id(2)
    hd = _DIFF_HD
    hw = 2 * hd

    q = q_ref[...]
    first = lax.broadcasted_iota(jnp.int32, (1, hw), 1) < hd
    q2_ref[0:tq, :] = jnp.where(first, q, 0.0).astype(_BF16)
    q2_ref[tq:2 * tq, :] = jnp.where(first, 0.0, q).astype(_BF16)
    m_ref[...] = jnp.full(m_ref.shape, _NEG, _F32)
    acc_ref[...] = jnp.zeros(acc_ref.shape, _F32)

    def scores(off, nk):
        return lax.dot_general(q2_ref[...], k_ref[pl.ds(off, nk), :], _NT, preferred_element_type=_F32)

    def softmax_pv(s, off, nk, visible):
        v_blk = v_ref[pl.ds(off, nk), :]
        v_ext = jnp.concatenate([v_blk, jnp.ones(v_blk.shape, v_blk.dtype)], axis=1)
        if visible is not None:
            s = jnp.where(visible, s, _NEG)
        m_old = m_ref[...]
        m_new = jnp.maximum(m_old, jnp.max(s, axis=-1, keepdims=True))
        alpha = jnp.exp2(m_old - m_new)
        if nk % _LANES == 0:
            p = jnp.exp2(s - jnp.concatenate([m_new] * (nk // _LANES), axis=1))
        else:
            p = jnp.exp2(s - m_new[:, 0:1])
        pv = jnp.dot(p.astype(_BF16), v_ext, preferred_element_type=_F32)
        acc_ref[...] = jnp.concatenate([alpha, alpha], axis=1) * acc_ref[...] + pv
        m_ref[...] = m_new

    q_start = p0 + qi * tq
    n_full = q_start // tk
    r_chunk = lax.broadcasted_iota(jnp.int32, (2 * tq, tq), 0) % tq // _CHUNK
    c_chunk = lax.broadcasted_iota(jnp.int32, (2 * tq, tq), 1) // _CHUNK
    vis = c_chunk <= r_chunk
    diag_off = pl.multiple_of(q_start, tq)

    if tq == tk:
        s_ref[0] = scores(0, tk)

        def body(j, carry):
            off0 = pl.multiple_of(2 * j * tk, tk)
            off1 = pl.multiple_of(off0 + tk, tk)
            s_ref[1] = scores(off1, tk)
            softmax_pv(s_ref[0], off0, tk, None)
            s_ref[0] = scores(pl.multiple_of(off1 + tk, tk), tk)
            softmax_pv(s_ref[1], off1, tk, None)
            return carry

        lax.fori_loop(0, n_full // 2, body, 0)

        @pl.when(n_full % 2 == 1)
        def _():
            s_ref[1] = scores(diag_off, tk)
            softmax_pv(s_ref[0], pl.multiple_of(diag_off - tk, tk), tk, None)
            softmax_pv(s_ref[1], diag_off, tq, vis)

        @pl.when(n_full % 2 == 0)
        def _():
            softmax_pv(s_ref[0], diag_off, tq, vis)
    else:
        def body(i, carry):
            off = pl.multiple_of(i * tk, tk)
            softmax_pv(scores(off, tk), off, tk, None)
            return carry

        lax.fori_loop(0, n_full, body, 0)
        softmax_pv(scores(diag_off, tq), diag_off, tq, vis)


    lp = lam_ref[...]
    lam = (jnp.exp(jnp.sum(lp[0:1] * lp[1:2], axis=-1, keepdims=True))
           - jnp.exp(jnp.sum(lp[2:3] * lp[3:4], axis=-1, keepdims=True)) + lam_init)
    acc = acc_ref[...]
    o = acc[0:tq, 0:hw] / acc[0:tq, hw:2 * hw] - lam * (acc[tq:2 * tq, 0:hw] / acc[tq:2 * tq, hw:2 * hw])
    o_ref[...] = (_rms(o) * sub_ref[...] * (1.0 - lam_init)).astype(o_ref.dtype)


def _diff_attention(q_arr, k_arr, v_arr, q_col, k_col, v_col, lam_p, subln, batch, t, p0, lam_init, n_heads, *, tq,
                    tk):
    hw = 2 * _DIFF_HD
    n_keys = p0 + t
    tq = _tile(t, tq)
    nq = t // tq
    tk = math.gcd(tk, p0, tq) if nq > 1 else math.gcd(tk, p0)
    assert tq % _CHUNK == 0 and tk % _CHUNK == 0
    kern = functools.partial(_diff_attn_kernel, p0=p0, tq=tq, tk=tk, lam_init=lam_init)
    return pl.pallas_call(
        kern,
        grid=(batch, n_heads, nq),
        in_specs=[
            pl.BlockSpec((tq, hw), lambda b, hh, qi: (b * nq + qi, q_col + hh)),
            pl.BlockSpec((n_keys, hw), lambda b, hh, qi: (b, k_col + hh)),
            pl.BlockSpec((n_keys, hw), lambda b, hh, qi: (b, v_col + hh)),
            pl.BlockSpec(lam_p.shape, lambda b, hh, qi: (0, 0)),
            pl.BlockSpec((1, hw), lambda b, hh, qi: (0, 0)),
        ],
        out_specs=pl.BlockSpec((tq, hw), lambda b, hh, qi: (b * nq + qi, hh)),
        out_shape=jax.ShapeDtypeStruct((batch * t, n_heads * hw), _BF16),
        scratch_shapes=[
            pltpu.VMEM((2 * tq, hw), _BF16),
            pltpu.VMEM((2 * tq, _LANES), _F32),
            pltpu.VMEM((2 * tq, 2 * hw), _F32),
            pltpu.VMEM((2, 2 * tq, tk) if tq == tk else (2, 8, _LANES), _F32),
        ],
        compiler_params=_params("parallel", "parallel", "arbitrary"),
        name="diff_attention",
    )(q_arr, k_arr, v_arr, lam_p, subln.reshape(1, hw))


def _router_kernel(x_ref, g_ref, wr_ref, idx_ref, wgt_ref):
    xn = _rms(x_ref[...]) * g_ref[...]
    logits = lax.dot_general(wr_ref[...], xn, _NT, preferred_element_type=_F32, precision=lax.Precision.HIGHEST)
    ne = logits.shape[0]
    row = lax.broadcasted_iota(jnp.int32, logits.shape, 0)
    v1 = jnp.max(logits, axis=0, keepdims=True)
    i1 = jnp.min(jnp.where(logits == v1, row, ne), axis=0, keepdims=True)
    rest = jnp.where(row == i1, -jnp.inf, logits)
    v2 = jnp.max(rest, axis=0, keepdims=True)
    i2 = jnp.min(jnp.where(rest == v2, row, ne), axis=0, keepdims=True)
    e2 = jnp.exp(v2 - v1)
    w1 = 1.0 / (1.0 + e2)
    idx_ref[...] = jnp.concatenate([i1, i2], axis=0)
    wgt_ref[...] = jnp.concatenate([w1, e2 * w1], axis=0)


def _router(x, gain, w_router_t, *, tm=512):
    n, d = x.shape
    ne = w_router_t.shape[0]
    tm = _tile(n, tm)
    return pl.pallas_call(
        _router_kernel,
        grid=(n // tm,),
        in_specs=[
            pl.BlockSpec((tm, d), lambda i: (i, 0)),
            pl.BlockSpec((1, d), lambda i: (0, 0)),
            pl.BlockSpec((ne, d), lambda i: (0, 0)),
        ],
        out_specs=[
            pl.BlockSpec((_TOP_K, tm), lambda i: (0, i)),
            pl.BlockSpec((_TOP_K, tm), lambda i: (0, i)),
        ],
        out_shape=[
            jax.ShapeDtypeStruct((_TOP_K, n), jnp.int32),
            jax.ShapeDtypeStruct((_TOP_K, n), _F32),
        ],
        compiler_params=_params("parallel"),
        name="moe_router",
    )(x, gain.reshape(1, d), w_router_t)


_GATHER_UNROLL = 8


def _start_row_gather(src_hbm, dst_vmem, idx_at, sem):
    def start(r, c):
        pltpu.make_async_copy(src_hbm.at[pl.ds(idx_at(r), 1), :], dst_vmem.at[pl.ds(r, 1), :], sem).start()
        return c

    lax.fori_loop(0, dst_vmem.shape[0], start, 0, unroll=_GATHER_UNROLL)


def _wait_row_gather(src_hbm, dst_vmem, sem):
    pltpu.make_async_copy(src_hbm.at[pl.ds(0, dst_vmem.shape[0]), :], dst_vmem, sem).wait()


def _expert_kernel(be_ref, nu_ref, rt_ref, rt_next_ref, x_hbm, g_ref, wg_ref, wu_ref, wd_ref, o_ref, xg_ref, xn_ref,
                   acc_ref, sem, *, prefetch_step):
    del be_ref
    i = pl.program_id(0)
    j = pl.program_id(1)
    n_used = nu_ref[0]
    active = i < n_used
    slot = lax.rem(i, 2)

    @pl.when(active & (i == 0) & (j == 0))
    def _():
        _start_row_gather(x_hbm, xg_ref.at[0], lambda r: rt_ref[0, 0, r], sem.at[0])

    @pl.when(active & (j == 0))
    def _():
        _wait_row_gather(x_hbm, xg_ref.at[slot], sem.at[slot])
        xn_ref[...] = (_rms(xg_ref[slot]) * g_ref[...]).astype(_BF16)
        acc_ref[...] = jnp.zeros(acc_ref.shape, _F32)

    @pl.when((i + 1 < n_used) & (j == prefetch_step))
    def _():
        _start_row_gather(x_hbm, xg_ref.at[1 - slot], lambda r: rt_next_ref[0, 0, r], sem.at[1 - slot])

    @pl.when(active)
    def _():
        xn = xn_ref[...]
        gate = jnp.dot(xn, wg_ref[0], preferred_element_type=_F32)
        up = jnp.dot(xn, wu_ref[0], preferred_element_type=_F32)
        act = (_silu(gate) * up).astype(_BF16)
        acc_ref[...] += jnp.dot(act, wd_ref[0], preferred_element_type=_F32)

    last = j == pl.num_programs(1) - 1

    @pl.when(active & last)
    def _():
        o_ref[...] = acc_ref[...]

    @pl.when(jnp.logical_not(active) & last)
    def _():
        o_ref[...] = jnp.zeros(o_ref.shape, _F32)


def _experts(x, gain, w_gu, w_down, row_tok, block_e, n_used, *, bm, tf=896):
    n, d = x.shape
    f = w_down.shape[1]
    tf = _tile(f, tf)
    nf = f // tf
    n_blocks = block_e.shape[0]

    def col(i, j, nu):
        return jnp.where(i < nu[0], j, nf - 1)

    grid_spec = pltpu.PrefetchScalarGridSpec(
        num_scalar_prefetch=2,
        grid=(n_blocks, nf),
        in_specs=[
            pl.BlockSpec((1, 1, bm), lambda i, j, be, nu: (i, 0, 0), memory_space=pltpu.SMEM),
            pl.BlockSpec((1, 1, bm), lambda i, j, be, nu: (jnp.minimum(i + 1, n_blocks - 1), 0, 0),
                         memory_space=pltpu.SMEM),
            pl.BlockSpec(memory_space=pl.ANY),
            pl.BlockSpec((1, d), lambda i, j, be, nu: (0, 0)),
            pl.BlockSpec((1, d, tf), lambda i, j, be, nu: (be[i], 0, col(i, j, nu))),
            pl.BlockSpec((1, d, tf), lambda i, j, be, nu: (be[i], 0, nf + col(i, j, nu))),
            pl.BlockSpec((1, tf, d), lambda i, j, be, nu: (be[i], col(i, j, nu), 0)),
        ],
        out_specs=pl.BlockSpec((bm, d), lambda i, j, be, nu: (i, 0)),
        scratch_shapes=[
            pltpu.VMEM((2, bm, d), _F32),
            pltpu.VMEM((bm, d), _BF16),
            pltpu.VMEM((bm, d), _F32),
            pltpu.SemaphoreType.DMA((2,)),
        ],
    )
    row_tok_b = row_tok.reshape(n_blocks, 1, bm)
    return pl.pallas_call(
        functools.partial(_expert_kernel, prefetch_step=min(1, nf - 1)),
        grid_spec=grid_spec,
        out_shape=jax.ShapeDtypeStruct((n_blocks * bm, d), _F32),
        compiler_params=_params("arbitrary", "arbitrary"),
        name="moe_experts",
    )(block_e, n_used, row_tok_b, row_tok_b, x, gain.reshape(1, d), w_gu, w_gu, w_down)


def _combine_kernel(d_ref, d_next_ref, x_ref, w_ref, y_hbm, gf_ref, o_ref, r_ref, sem):
    i = pl.program_id(0)
    slot = lax.rem(i, 2)

    def start(dref, s):
        for kk in range(_TOP_K):
            _start_row_gather(y_hbm, r_ref.at[s, kk], lambda r, kk=kk: dref[0, kk, r], sem.at[s])

    @pl.when(i == 0)
    def _():
        start(d_ref, 0)

    @pl.when(i + 1 < pl.num_programs(0))
    def _():
        start(d_next_ref, 1 - slot)

    for kk in range(_TOP_K):
        _wait_row_gather(y_hbm, r_ref.at[slot, kk], sem.at[slot])
    w = w_ref[...]
    x = x_ref[...]
    for kk in range(_TOP_K):
        x = x + w[:, kk:kk + 1] * r_ref[slot, kk]
    o_ref[...] = _rms(x) * gf_ref[...]


def _combine_final(x, dest, wgt, yb, gain_final, *, tm=256):
    n, d = x.shape
    tm = _tile(n, tm)
    nt = n // tm
    dest_b = dest.reshape(nt, tm, _TOP_K).transpose(0, 2, 1)
    return pl.pallas_call(
        _combine_kernel,
        grid=(nt,),
        in_specs=[
            pl.BlockSpec((1, _TOP_K, tm), lambda i: (i, 0, 0), memory_space=pltpu.SMEM),
            pl.BlockSpec((1, _TOP_K, tm), lambda i: (jnp.minimum(i + 1, nt - 1), 0, 0), memory_space=pltpu.SMEM),
            pl.BlockSpec((tm, d), lambda i: (i, 0)),
            pl.BlockSpec((tm, _TOP_K), lambda i: (i, 0)),
            pl.BlockSpec(memory_space=pl.ANY),
            pl.BlockSpec((1, d), lambda i: (0, 0)),
        ],
        out_specs=pl.BlockSpec((tm, d), lambda i: (i, 0)),
        out_shape=jax.ShapeDtypeStruct((n, d), _F32),
        scratch_shapes=[
            pltpu.VMEM((2, _TOP_K, tm, d), _F32),
            pltpu.SemaphoreType.DMA((2,)),
        ],
        compiler_params=_params("arbitrary"),
        name="moe_combine_final_norm",
    )(dest_b, dest_b, x, wgt, yb, gain_final.reshape(1, d))


def _moe_final(x, gain, w_router, w_gu, w_down, gain_final, *, bm):
    n, d = x.shape
    ne = w_router.shape[1]
    idx, wgt = _router(x, gain, w_router.T)
    a = n * _TOP_K
    e_flat = idx.T.reshape(a)
    onehot = (e_flat[:, None] == jnp.arange(ne, dtype=jnp.int32)[None, :]).astype(jnp.int32)
    csum = jnp.cumsum(onehot, axis=0)
    rank = jnp.sum(csum * onehot, axis=1) - 1
    counts = csum[-1]
    padded = (counts + bm - 1) // bm * bm
    pad_end = jnp.cumsum(padded)
    pad_start = pad_end - padded
    dest = pad_start[e_flat] + rank
    n_blocks = -(-a // bm) + ne
    row_tok = jnp.zeros((n_blocks * bm,), jnp.int32).at[dest].set(jnp.arange(a, dtype=jnp.int32) // _TOP_K)
    block_start = jnp.arange(n_blocks, dtype=jnp.int32) * bm
    block_e = jnp.minimum(jnp.sum((block_start[:, None] >= pad_end[None, :]).astype(jnp.int32), axis=1), ne - 1)
    n_used = (pad_end[-1:] // bm).astype(jnp.int32)
    yb = _experts(x, gain, w_gu, w_down, row_tok, block_e, n_used, bm=bm)
    return _combine_final(x, dest.reshape(n, _TOP_K), wgt.T, yb, gain_final)


def _rope_tables(pos, dim, lanes_signed):
    half = dim // 2
    inv = _ROPE_THETA ** (-jnp.arange(half, dtype=_F32) / half)
    ang = pos.astype(_F32)[:, None] * inv[None, :]
    cos, sin = jnp.cos(ang), jnp.sin(ang)
    if not lanes_signed:
        return cos, sin
    reps = _LANES // dim
    cos_l = jnp.tile(jnp.concatenate([cos, cos], axis=1), (1, reps))
    sin_l = jnp.tile(jnp.concatenate([-sin, sin], axis=1), (1, reps))
    return cos_l, sin_l


def _trunk(x, pos, p0, ret_s0, past_k, past_v, mem_k, mem_v, w, *, big):
    b, t, d = x.shape
    n = b * t
    xf = x.reshape(n, d)
    tm = 512 if big else 256
    dk = d // _RET_HEADS
    hd_all = d

    cos, sin = _rope_tables(pos, dk, False)
    if cos.shape[0] % tm != 0:
        cos, sin = jnp.tile(cos, (b, 1)), jnp.tile(sin, (b, 1))
    (qkvg,) = _proj(xf, w["norm_mix"][0], w["ret_w_in"], cos, sin, outs=[(_BF16, 0, w["ret_w_in"].shape[1])],
                    half=dk // 2, n_rope_cols=2 * d, n_first_cols=d, scale_first=1.0, scale_rest=dk ** -0.5,
                    tm=2 * tm, tn=1024)
    o, ret_state = _retention(qkvg, ret_s0, b, t)
    xf = _out_proj(o, w["ret_w_out"], xf, tm=tm)
    xf = _cross_attention(xf, w["norm_cross"][0], w["cross_wq"][0], w["cross_wo"][0],
                          mem_k[0].reshape(b, -1, d), mem_v[0].reshape(b, -1, d), b, t, tm=tm)
    xf = _dense_ffn(xf, w["norm_ffn"][0], w["ffn_w_gu"], w["ffn_w_down"], tm=tm)

    lam_init = 0.8 - 0.6 * math.exp(-0.3 * 1)
    cos, sin = _rope_tables(pos, _DIFF_HD, True)
    if cos.shape[0] % tm != 0:
        cos, sin = jnp.tile(cos, (b, 1)), jnp.tile(sin, (b, 1))
    qkv, k_rows, v_rows = _proj(
        xf, w["norm_mix"][1], w["diff_w_in"], cos, sin,
        outs=[(_BF16, 0, 3 * hd_all), (_F32, hd_all, 2 * hd_all), (_F32, 2 * hd_all, 3 * hd_all)],
        half=_DIFF_HD // 2, n_rope_cols=2 * hd_all, n_first_cols=hd_all,
        scale_first=_DIFF_HD ** -0.5 * math.log2(math.e),
        scale_rest=1.0, tm=tm, tn=1024)
    n_heads = d // (2 * _DIFF_HD)
    if past_k is None:
        o = _diff_attention(qkv, qkv, qkv, 0, n_heads, 2 * n_heads, w["diff_lambda"], w["diff_subln"], b, t, p0,
                            lam_init, n_heads, tq=512, tk=512)
    else:
        n_past = past_k.shape[1]
        n_keys = n_past + t
        k_new = qkv[:, hd_all:2 * hd_all].reshape(b, t, d)
        v_new = qkv[:, 2 * hd_all:].reshape(b, t, d)
        k_all = jnp.concatenate([past_k.reshape(b, n_past, d).astype(_BF16), k_new], axis=1)
        v_all = jnp.concatenate([past_v.reshape(b, n_past, d).astype(_BF16), v_new], axis=1)
        o = _diff_attention(qkv, k_all.reshape(b * n_keys, d), v_all.reshape(b * n_keys, d), 0, 0, 0,
                            w["diff_lambda"], w["diff_subln"], b, t, p0, lam_init, n_heads, tq=t, tk=512)
    xf = _out_proj(o, w["diff_w_out"], xf, tm=tm)
    xf = _cross_attention(xf, w["norm_cross"][1], w["cross_wq"][1], w["cross_wo"][1],
                          mem_k[1].reshape(b, -1, d), mem_v[1].reshape(b, -1, d), b, t, tm=tm)
    y = _moe_final(xf, w["norm_ffn"][1], w["moe_router"], w["moe_w_gu"], w["moe_w_down"], w["norm_final"],
                   bm=512 if big else 128)
    hw = 2 * _DIFF_HD
    return (y.reshape(b, t, d), ret_state, k_rows.reshape(b, t, n_heads, hw), v_rows.reshape(b, t, n_heads, hw))


def kernel(x_prompt, x_sample, state_ret, cache_diff_k, cache_diff_v, cache_mem_k, cache_mem_v, mem_prompt, norm_mix,
           norm_cross, norm_mem, norm_ffn, norm_final, ret_w_in, ret_w_out, diff_w_in, diff_lambda, diff_subln,
           diff_w_out, cross_wq, cross_wkv, cross_wo, ffn_w_gu, ffn_w_down, moe_router, moe_w_gu, moe_w_down):
    depth = norm_mix.shape[0]
    assert depth == 2, "one retention layer followed by one differential-attention layer"
    bf = lambda a: a.astype(_BF16)
    w = dict(norm_mix=norm_mix, norm_cross=norm_cross, norm_ffn=norm_ffn, norm_final=norm_final,
             ret_w_in=bf(ret_w_in[0]), ret_w_out=bf(ret_w_out[0]), diff_w_in=bf(diff_w_in[0]),
             diff_lambda=diff_lambda[0], diff_subln=diff_subln[0], diff_w_out=bf(diff_w_out[0]),
             cross_wq=bf(cross_wq), cross_wo=bf(cross_wo), ffn_w_gu=bf(ffn_w_gu[0]), ffn_w_down=bf(ffn_w_down[0]),
             moe_router=moe_router[0], moe_w_gu=bf(moe_w_gu[0]), moe_w_down=bf(moe_w_down[0]))

    b_p, t_p, d = x_prompt.shape
    b_s, t_s, _ = x_sample.shape
    mem_len = mem_prompt.shape[1]
    mem_flat = mem_prompt.reshape(b_p * mem_len, d)
    mks, mvs = [], []
    for i in range(depth):
        mk, mv = _proj(mem_flat, norm_mem[i], bf(cross_wkv[i]), None, None, outs=[(_F32, 0, d), (_F32, d, 2 * d)],
                       tm=256)
        mks.append(mk.reshape(b_p, mem_len, _CROSS_HEADS, d // _CROSS_HEADS))
        mvs.append(mv.reshape(b_p, mem_len, _CROSS_HEADS, d // _CROSS_HEADS))
    mem_k_p, mem_v_p = jnp.stack(mks), jnp.stack(mvs)

    dkr = d // _RET_HEADS
    s0 = jnp.zeros((b_p, _RET_HEADS, dkr, 2 * dkr), _F32)
    y_p, rs_p, dk_p, dv_p = _trunk(x_prompt, jnp.arange(t_p, dtype=jnp.int32), 0, s0, None, None, mem_k_p, mem_v_p, w,
                                   big=True)
    n_past = cache_diff_k.shape[2]
    y_s, rs_s, dk_s, dv_s = _trunk(x_sample, n_past + jnp.arange(t_s, dtype=jnp.int32), n_past, state_ret[0],
                                   cache_diff_k[0], cache_diff_v[0], cache_mem_k, cache_mem_v, w, big=False)
    return (y_p, y_s, rs_p[None], dk_p[None], dv_p[None], mem_k_p, mem_v_p, rs_s[None], dk_s[None], dv_s[None])
```

```python
import functools
import math

import jax
import jax.numpy as jnp
from jax import lax
from jax.experimental import pallas as pl
from jax.experimental.pallas import tpu as pltpu

_CHUNK = 64
_ROPE_THETA = 10000.0
_EPS = 1e-6
_RET_HEADS = 4
_DIFF_HD = 64
_CROSS_HEADS = 4
_N_EXPERTS = 8
_TOP_K = 2

_VMEM_LIMIT_V7X = 56 * 1024 * 1024
_LANES = 128

_F32 = jnp.float32
_BF16 = jnp.bfloat16
_NT = (((1,), (1,)), ((), ()))
_TN = (((0,), (0,)), ((), ()))
_NEG = -1e30


def _params(*sem):
    return pltpu.CompilerParams(dimension_semantics=sem, vmem_limit_bytes=_VMEM_LIMIT_V7X)


def _tile(n, pref):
    t = min(n, pref)
    assert n % t == 0, (n, pref)
    return t


def _rms(x):
    return x * lax.rsqrt(jnp.mean(x * x, axis=-1, keepdims=True) + _EPS)


def _silu(x):
    return x * jax.nn.sigmoid(x)


def _proj_kernel(x_ref, g_ref, w_ref, cos_ref, sin_ref, *rest, half, n_rope, n_first, scale_first, scale_rest,
                 out_ranges):
    o_refs, xn_ref = rest[:-1], rest[-1]
    j = pl.program_id(1)

    @pl.when(j == 0)
    def _():
        xn_ref[...] = (_rms(x_ref[...]) * g_ref[...]).astype(_BF16)

    acc = jnp.dot(xn_ref[...], w_ref[...], preferred_element_type=_F32)
    tn = acc.shape[1]

    def emit(val, lo, hi):
        for o_ref, (olo, ohi) in zip(o_refs, out_ranges):
            if olo >= hi or ohi <= lo:
                continue
            if olo <= lo and hi <= ohi:
                o_ref[...] = val.astype(o_ref.dtype)
            else:
                @pl.when((j >= olo) & (j < ohi))
                def _(o_ref=o_ref):
                    o_ref[...] = val.astype(o_ref.dtype)

    if n_rope > 0:
        @pl.when(j < n_rope)
        def _():
            cos = cos_ref[...]
            sin = sin_ref[...]
            pieces = []
            if half == _LANES:
                for c in range(tn // (2 * _LANES)):
                    x1 = acc[:, 2 * c * _LANES:(2 * c + 1) * _LANES]
                    x2 = acc[:, (2 * c + 1) * _LANES:(2 * c + 2) * _LANES]
                    pieces += [x1 * cos - x2 * sin, x2 * cos + x1 * sin]
            else:
                lane = lax.broadcasted_iota(jnp.int32, (1, _LANES), 1)
                first = (lane % (2 * half)) < half
                for c in range(tn // _LANES):
                    xc = acc[:, c * _LANES:(c + 1) * _LANES]
                    rot = jnp.where(first, pltpu.roll(xc, _LANES - half, 1), pltpu.roll(xc, half, 1))
                    pieces.append(xc * cos + rot * sin)
            r = jnp.concatenate(pieces, axis=1)
            scale = jnp.where(j < n_first, scale_first, scale_rest).astype(_F32)
            emit(r * scale, 0, n_rope)

        @pl.when(j >= n_rope)
        def _():
            emit(acc, n_rope, 1 << 30)
    else:
        emit(acc, 0, 1 << 30)


def _proj(x, gain, w, cos, sin, *, outs, half=_LANES, n_rope_cols=0, n_first_cols=0, scale_first=1.0,
          scale_rest=1.0, tm=512, tn=512):
    n, d = x.shape
    d_out = w.shape[1]
    tm = _tile(n, tm)
    tn = _tile(d_out, tn)
    assert n_rope_cols % tn == 0 and n_first_cols % tn == 0
    if cos is None:
        cos = jnp.zeros((tm, _LANES), _F32)
        sin = cos
    r_tiles = cos.shape[0] // tm
    assert cos.shape[0] % tm == 0
    out_ranges = []
    for _, lo, hi in outs:
        assert lo % tn == 0 and hi % tn == 0
        out_ranges.append((lo // tn, hi // tn))
    kern = functools.partial(_proj_kernel, half=half, n_rope=n_rope_cols // tn, n_first=n_first_cols // tn,
                             scale_first=scale_first, scale_rest=scale_rest, out_ranges=tuple(out_ranges))

    def out_map(lo, hi):
        return lambda i, j: (i, jnp.clip(j - lo, 0, hi - lo - 1))

    return pl.pallas_call(
        kern,
        grid=(n // tm, d_out // tn),
        in_specs=[
            pl.BlockSpec((tm, d), lambda i, j: (i, 0)),
            pl.BlockSpec((1, d), lambda i, j: (0, 0)),
            pl.BlockSpec((d, tn), lambda i, j: (0, j)),
            pl.BlockSpec((tm, _LANES), lambda i, j: (i % r_tiles, 0)),
            pl.BlockSpec((tm, _LANES), lambda i, j: (i % r_tiles, 0)),
        ],
        out_specs=[pl.BlockSpec((tm, tn), out_map(lo, hi)) for lo, hi in out_ranges],
        out_shape=[jax.ShapeDtypeStruct((n, hi - lo), dt) for dt, lo, hi in outs],
        scratch_shapes=[pltpu.VMEM((tm, d), _BF16)],
        compiler_params=_params("parallel", "arbitrary"),
        name="norm_proj",
    )(x, gain.reshape(1, d), w, cos, sin)


def _out_proj_kernel(a_ref, w_ref, r_ref, o_ref):
    o_ref[...] = r_ref[...] + jnp.dot(a_ref[...], w_ref[...], preferred_element_type=_F32)


def _out_proj(a, w, res, *, tm=512):
    n, k = a.shape
    d = w.shape[1]
    tm = _tile(n, tm)
    return pl.pallas_call(
        _out_proj_kernel,
        grid=(n // tm,),
        in_specs=[
            pl.BlockSpec((tm, k), lambda i: (i, 0)),
            pl.BlockSpec((k, d), lambda i: (0, 0)),
            pl.BlockSpec((tm, d), lambda i: (i, 0)),
        ],
        out_specs=pl.BlockSpec((tm, d), lambda i: (i, 0)),
        out_shape=jax.ShapeDtypeStruct((n, d), _F32),
        compiler_params=_params("parallel"),
        name="out_proj_residual",
    )(a, w, res)


def _retention_kernel(q_ref, k_ref, v_ref, g_ref, s0_ref, dm_ref, qd_ref, kd_ref, cd_ref, o_ref, so_ref, s_ref):
    n = pl.program_id(1)
    heads, dk, dv = s_ref.shape

    @pl.when(n == 0)
    def _():
        s_ref[...] = s0_ref[0]

    for hh in range(heads):
        q = q_ref[:, hh * dk:(hh + 1) * dk]
        k = k_ref[:, hh * dk:(hh + 1) * dk]
        v = v_ref[:, hh * dv:(hh + 1) * dv]
        sc = lax.dot_general(q, k, _NT, preferred_element_type=_F32)
        p = (sc * dm_ref[hh]).astype(_BF16)
        s_old = s_ref[hh]
        q_dec = (q.astype(_F32) * qd_ref[hh]).astype(_BF16)
        o = jnp.dot(p, v, preferred_element_type=_F32) + jnp.dot(q_dec, s_old.astype(_BF16),
                                                                preferred_element_type=_F32)
        k_dec = (k.astype(_F32) * kd_ref[hh]).astype(_BF16)
        s_ref[hh] = s_old * cd_ref[hh] + lax.dot_general(k_dec, v, _TN, preferred_element_type=_F32)
        g = g_ref[:, hh * dv:(hh + 1) * dv].astype(_F32)
        o_ref[:, hh * dv:(hh + 1) * dv] = (_silu(g) * _rms(o)).astype(o_ref.dtype)

    @pl.when(n == pl.num_programs(1) - 1)
    def _():
        so_ref[0] = s_ref[...]


def _retention(qkvg, s0, batch, t, *, block=256):
    h = _RET_HEADS
    dk, dv = s0.shape[2], s0.shape[3]
    lb = _tile(t, block)
    nb = t // lb
    assert lb % _CHUNK == 0
    log_g = jnp.log1p(-(2.0 ** (-5.0 - jnp.arange(h, dtype=_F32))))
    i = jnp.arange(lb, dtype=_F32)
    ci = jnp.arange(lb) // _CHUNK
    diff = i[:, None] - i[None, :]
    same = ci[:, None] == ci[None, :]
    before = ci[None, :] < ci[:, None]
    expo = jnp.where(same, jnp.abs(diff), diff)
    dmask = jnp.where((same | before)[None], jnp.exp(expo[None] * log_g[:, None, None]), 0.0).astype(_F32)
    q_dec = jnp.exp((i[None, :] + 1.0) * log_g[:, None])[..., None]
    k_dec = jnp.exp((lb - 1.0 - i)[None, :] * log_g[:, None])[..., None]
    c_dec = jnp.exp(lb * log_g).reshape(h, 1, 1)
    wq, wv = h * dk, h * dv
    assert (2 * wq) % wv == 0
    v_blk = (2 * wq) // wv
    return pl.pallas_call(
        _retention_kernel,
        grid=(batch, nb),
        in_specs=[
            pl.BlockSpec((lb, wq), lambda b, n: (b * nb + n, 0)),
            pl.BlockSpec((lb, wq), lambda b, n: (b * nb + n, 1)),
            pl.BlockSpec((lb, wv), lambda b, n: (b * nb + n, v_blk)),
            pl.BlockSpec((lb, wv), lambda b, n: (b * nb + n, v_blk + 1)),
            pl.BlockSpec((1, h, dk, dv), lambda b, n: (b, 0, 0, 0)),
            pl.BlockSpec((h, lb, lb), lambda b, n: (0, 0, 0)),
            pl.BlockSpec((h, lb, 1), lambda b, n: (0, 0, 0)),
            pl.BlockSpec((h, lb, 1), lambda b, n: (0, 0, 0)),
            pl.BlockSpec((h, 1, 1), lambda b, n: (0, 0, 0)),
        ],
        out_specs=[
            pl.BlockSpec((lb, wv), lambda b, n: (b * nb + n, 0)),
            pl.BlockSpec((1, h, dk, dv), lambda b, n: (b, 0, 0, 0)),
        ],
        out_shape=[
            jax.ShapeDtypeStruct((batch * t, wv), _BF16),
            jax.ShapeDtypeStruct((batch, h, dk, dv), _F32),
        ],
        scratch_shapes=[pltpu.VMEM((h, dk, dv), _F32)],
        compiler_params=_params("parallel", "arbitrary"),
        name="retention",
    )(qkvg, qkvg, qkvg, qkvg, s0, dmask, q_dec, k_dec, c_dec)


def _cross_kernel(x_ref, g_ref, wq_ref, wo_ref, mk_ref, mv_ref, o_ref, *, heads):
    x = x_ref[...]
    d = x.shape[1]
    hd = d // heads
    xn = (_rms(x) * g_ref[...]).astype(_BF16)
    q = (jnp.dot(xn, wq_ref[...], preferred_element_type=_F32) * (hd ** -0.5)).astype(_BF16)
    mk = mk_ref[0].astype(_BF16)
    mv = mv_ref[0].astype(_BF16)
    outs = []
    for hh in range(heads):
        sl = slice(hh * hd, (hh + 1) * hd)
        s = lax.dot_general(q[:, sl], mk[:, sl], _NT, preferred_element_type=_F32)
        s = s - jnp.max(s, axis=-1, keepdims=True)
        e = jnp.exp(s)
        p = (e / jnp.sum(e, axis=-1, keepdims=True)).astype(_BF16)
        outs.append(jnp.dot(p, mv[:, sl], preferred_element_type=_F32).astype(_BF16))
    o = jnp.concatenate(outs, axis=1)
    o_ref[...] = x + jnp.dot(o, wo_ref[...], preferred_element_type=_F32)


def _cross_attention(x, gain, wq, wo, mk, mv, batch, t, *, tm=512):
    n, d = x.shape
    tm = _tile(t, tm)
    nt = t // tm
    mem = mk.shape[1]
    return pl.pallas_call(
        functools.partial(_cross_kernel, heads=_CROSS_HEADS),
        grid=(batch, nt),
        in_specs=[
            pl.BlockSpec((tm, d), lambda b, i: (b * nt + i, 0)),
            pl.BlockSpec((1, d), lambda b, i: (0, 0)),
            pl.BlockSpec((d, d), lambda b, i: (0, 0)),
            pl.BlockSpec((d, d), lambda b, i: (0, 0)),
            pl.BlockSpec((1, mem, d), lambda b, i: (b, 0, 0)),
            pl.BlockSpec((1, mem, d), lambda b, i: (b, 0, 0)),
        ],
        out_specs=pl.BlockSpec((tm, d), lambda b, i: (b * nt + i, 0)),
        out_shape=jax.ShapeDtypeStruct((n, d), _F32),
        compiler_params=_params("parallel", "parallel"),
        name="cross_attention",
    )(x, gain.reshape(1, d), wq, wo, mk, mv)


def _ffn_kernel(x_ref, g_ref, wg_ref, wu_ref, wd_ref, o_ref, xn_ref, acc_ref):
    j = pl.program_id(1)

    @pl.when(j == 0)
    def _():
        xn_ref[...] = (_rms(x_ref[...]) * g_ref[...]).astype(_BF16)
        acc_ref[...] = x_ref[...]

    xn = xn_ref[...]
    gate = jnp.dot(xn, wg_ref[...], preferred_element_type=_F32)
    up = jnp.dot(xn, wu_ref[...], preferred_element_type=_F32)
    act = (_silu(gate) * up).astype(_BF16)
    acc_ref[...] += jnp.dot(act, wd_ref[...], preferred_element_type=_F32)

    @pl.when(j == pl.num_programs(1) - 1)
    def _():
        o_ref[...] = acc_ref[...]


def _dense_ffn(x, gain, w_gu, w_down, *, tm=512, tf=1408):
    n, d = x.shape
    f = w_down.shape[0]
    tm = _tile(n, tm)
    tf = _tile(f, tf)
    nf = f // tf
    return pl.pallas_call(
        _ffn_kernel,
        grid=(n // tm, nf),
        in_specs=[
            pl.BlockSpec((tm, d), lambda i, j: (i, 0)),
            pl.BlockSpec((1, d), lambda i, j: (0, 0)),
            pl.BlockSpec((d, tf), lambda i, j: (0, j)),
            pl.BlockSpec((d, tf), lambda i, j: (0, nf + j)),
            pl.BlockSpec((tf, d), lambda i, j: (j, 0)),
        ],
        out_specs=pl.BlockSpec((tm, d), lambda i, j: (i, 0)),
        out_shape=jax.ShapeDtypeStruct((n, d), _F32),
        scratch_shapes=[pltpu.VMEM((tm, d), _BF16), pltpu.VMEM((tm, d), _F32)],
        compiler_params=_params("parallel", "arbitrary"),
        name="dense_swiglu",
    )(x, gain.reshape(1, d), w_gu, w_gu, w_down)


def _diff_attn_kernel(q_ref, k_ref, v_ref, lam_ref, sub_ref, o_ref, q2_ref, m_ref, acc_ref, s_ref, *, p0, tq, tk,
                      lam_init):
    qi = pl.program_id(2)
    hd = _DIFF_HD
    hw = 2 * hd

    q = q_ref[...]
    first = lax.broadcasted_iota(jnp.int32, (1, hw), 1) < hd
    q2_ref[0:tq, :] = jnp.where(first, q, 0.0).astype(_BF16)
    q2_ref[tq:2 * tq, :] = jnp.where(first, 0.0, q).astype(_BF16)
    m_ref[...] = jnp.full(m_ref.shape, _NEG, _F32)
    acc_ref[...] = jnp.zeros(acc_ref.shape, _F32)

    def scores(off, nk):
        return lax.dot_general(q2_ref[...], k_ref[pl.ds(off, nk), :], _NT, preferred_element_type=_F32)

    def softmax_pv(s, off, nk, visible):
        v_blk = v_ref[pl.ds(off, nk), :]
        v_ext = jnp.concatenate([v_blk, jnp.ones(v_blk.shape, v_blk.dtype)], axis=1)
        if visible is not None:
            s = jnp.where(visible, s, _NEG)
        m_old = m_ref[...]
        m_new = jnp.maximum(m_old, jnp.max(s, axis=-1, keepdims=True))
        alpha = jnp.exp2(m_old - m_new)
        if nk % _LANES == 0:
            p = jnp.exp2(s - jnp.concatenate([m_new] * (nk // _LANES), axis=1))
        else:
            p = jnp.exp2(s - m_new[:, 0:1])
        pv = jnp.dot(p.astype(_BF16), v_ext, preferred_element_type=_F32)
        acc_ref[...] = jnp.concatenate([alpha, alpha], axis=1) * acc_ref[...] + pv
        m_ref[...] = m_new

    q_start = p0 + qi * tq
    n_full = q_start // tk
    r_chunk = lax.broadcasted_iota(jnp.int32, (2 * tq, tq), 0) % tq // _CHUNK
    c_chunk = lax.broadcasted_iota(jnp.int32, (2 * tq, tq), 1) // _CHUNK
    vis = c_chunk <= r_chunk
    diag_off = pl.multiple_of(q_start, tq)

    if tq == tk:
        s_ref[0] = scores(0, tk)

        def body(j, carry):
            off0 = pl.multiple_of(2 * j * tk, tk)
            off1 = pl.multiple_of(off0 + tk, tk)
            s_ref[1] = scores(off1, tk)
            softmax_pv(s_ref[0], off0, tk, None)
            s_ref[0] = scores(pl.multiple_of(off1 + tk, tk), tk)
            softmax_pv(s_ref[1], off1, tk, None)
            return carry

        lax.fori_loop(0, n_full // 2, body, 0)

        @pl.when(n_full % 2 == 1)
        def _():
            s_ref[1] = scores(diag_off, tk)
            softmax_pv(s_ref[0], pl.multiple_of(diag_off - tk, tk), tk, None)
            softmax_pv(s_ref[1], diag_off, tq, vis)

        @pl.when(n_full % 2 == 0)
        def _():
            softmax_pv(s_ref[0], diag_off, tq, vis)
    else:
        def body(i, carry):
            off = pl.multiple_of(i * tk, tk)
            softmax_pv(scores(off, tk), off, tk, None)
            return carry

        lax.fori_loop(0, n_full, body, 0)
        softmax_pv(scores(diag_off, tq), diag_off, tq, vis)


    lp = lam_ref[...]
    lam = (jnp.exp(jnp.sum(lp[0:1] * lp[1:2], axis=-1, keepdims=True))
           - jnp.exp(jnp.sum(lp[2:3] * lp[3:4], axis=-1, keepdims=True)) + lam_init)
    acc = acc_ref[...]
    o = acc[0:tq, 0:hw] / acc[0:tq, hw:2 * hw] - lam * (acc[tq:2 * tq, 0:hw] / acc[tq:2 * tq, hw:2 * hw])
    o_ref[...] = (_rms(o) * sub_ref[...] * (1.0 - lam_init)).astype(o_ref.dtype)


def _diff_attention(q_arr, k_arr, v_arr, q_col, k_col, v_col, lam_p, subln, batch, t, p0, lam_init, n_heads, *, tq,
                    tk):
    hw = 2 * _DIFF_HD
    n_keys = p0 + t
    tq = _tile(t, tq)
    nq = t // tq
    tk = math.gcd(tk, p0, tq) if nq > 1 else math.gcd(tk, p0)
    assert tq % _CHUNK == 0 and tk % _CHUNK == 0
    kern = functools.partial(_diff_attn_kernel, p0=p0, tq=tq, tk=tk, lam_init=lam_init)
    return pl.pallas_call(
        kern,
        grid=(batch, n_heads, nq),
        in_specs=[
            pl.BlockSpec((tq, hw), lambda b, hh, qi: (b * nq + qi, q_col + hh)),
            pl.BlockSpec((n_keys, hw), lambda b, hh, qi: (b, k_col + hh)),
            pl.BlockSpec((n_keys, hw), lambda b, hh, qi: (b, v_col + hh)),
            pl.BlockSpec(lam_p.shape, lambda b, hh, qi: (0, 0)),
            pl.BlockSpec((1, hw), lambda b, hh, qi: (0, 0)),
        ],
        out_specs=pl.BlockSpec((tq, hw), lambda b, hh, qi: (b * nq + qi, hh)),
        out_shape=jax.ShapeDtypeStruct((batch * t, n_heads * hw), _BF16),
        scratch_shapes=[
            pltpu.VMEM((2 * tq, hw), _BF16),
            pltpu.VMEM((2 * tq, _LANES), _F32),
            pltpu.VMEM((2 * tq, 2 * hw), _F32),
            pltpu.VMEM((2, 2 * tq, tk) if tq == tk else (2, 8, _LANES), _F32),
        ],
        compiler_params=_params("parallel", "parallel", "arbitrary"),
        name="diff_attention",
    )(q_arr, k_arr, v_arr, lam_p, subln.reshape(1, hw))


def _router_kernel(x_ref, g_ref, wr_ref, idx_ref, wgt_ref):
    xn = _rms(x_ref[...]) * g_ref[...]
    logits = lax.dot_general(wr_ref[...], xn, _NT, preferred_element_type=_F32, precision=lax.Precision.HIGHEST)
    ne = logits.shape[0]
    row = lax.broadcasted_iota(jnp.int32, logits.shape, 0)
    v1 = jnp.max(logits, axis=0, keepdims=True)
    i1 = jnp.min(jnp.where(logits == v1, row, ne), axis=0, keepdims=True)
    rest = jnp.where(row == i1, -jnp.inf, logits)
    v2 = jnp.max(rest, axis=0, keepdims=True)
    i2 = jnp.min(jnp.where(rest == v2, row, ne), axis=0, keepdims=True)
    e2 = jnp.exp(v2 - v1)
    w1 = 1.0 / (1.0 + e2)
    idx_ref[...] = jnp.concatenate([i1, i2], axis=0)
    wgt_ref[...] = jnp.concatenate([w1, e2 * w1], axis=0)


def _router(x, gain, w_router_t, *, tm=512):
    n, d = x.shape
    ne = w_router_t.shape[0]
    tm = _tile(n, tm)
    return pl.pallas_call(
        _router_kernel,
        grid=(n // tm,),
        in_specs=[
            pl.BlockSpec((tm, d), lambda i: (i, 0)),
            pl.BlockSpec((1, d), lambda i: (0, 0)),
            pl.BlockSpec((ne, d), lambda i: (0, 0)),
        ],
        out_specs=[
            pl.BlockSpec((_TOP_K, tm), lambda i: (0, i)),
            pl.BlockSpec((_TOP_K, tm), lambda i: (0, i)),
        ],
        out_shape=[
            jax.ShapeDtypeStruct((_TOP_K, n), jnp.int32),
            jax.ShapeDtypeStruct((_TOP_K, n), _F32),
        ],
        compiler_params=_params("parallel"),
        name="moe_router",
    )(x, gain.reshape(1, d), w_router_t)


_GATHER_UNROLL = 8


def _start_row_gather(src_hbm, dst_vmem, idx_at, sem, *, static_rows=False):
    def start(r, c):
        pltpu.make_async_copy(src_hbm.at[pl.ds(idx_at(r), 1), :], dst_vmem.at[pl.ds(r, 1), :], sem).start()
        return c

    if static_rows:
        for r in range(dst_vmem.shape[0]):
            start(r, 0)
    else:
        lax.fori_loop(0, dst_vmem.shape[0], start, 0, unroll=_GATHER_UNROLL)


def _wait_row_gather(src_hbm, dst_vmem, sem):
    pltpu.make_async_copy(src_hbm.at[pl.ds(0, dst_vmem.shape[0]), :], dst_vmem, sem).wait()


def _expert_kernel(be_ref, nu_ref, rt_ref, rt_next_ref, x_hbm, g_ref, wg_ref, wu_ref, wd_ref, o_ref, xg_ref, xn_ref,
                   acc_ref, sem, *, rows_per_step):
    del be_ref
    i = pl.program_id(0)
    j = pl.program_id(1)
    n_used = nu_ref[0]
    active = i < n_used
    slot = lax.rem(i, 2)

    @pl.when(active & (i == 0) & (j == 0))
    def _():
        _start_row_gather(x_hbm, xg_ref.at[0], lambda r: rt_ref[0, 0, r], sem.at[0])

    @pl.when((i <= n_used) & (j == 0))
    def _():
        _wait_row_gather(x_hbm, xg_ref.at[slot], sem.at[slot])

    @pl.when(active & (j == 0))
    def _():
        xn_ref[...] = (_rms(xg_ref[slot]) * g_ref[...]).astype(_BF16)
        acc_ref[...] = jnp.zeros(acc_ref.shape, _F32)

    @pl.when(active)
    def _():
        base = j * rows_per_step
        for c in range(rows_per_step):
            r = base + c
            pltpu.make_async_copy(x_hbm.at[pl.ds(rt_next_ref[0, 0, r], 1), :],
                                  xg_ref.at[1 - slot, pl.ds(r, 1), :], sem.at[1 - slot]).start()
        xn = xn_ref[...]
        gate = jnp.dot(xn, wg_ref[0], preferred_element_type=_F32)
        up = jnp.dot(xn, wu_ref[0], preferred_element_type=_F32)
        act = (_silu(gate) * up).astype(_BF16)
        acc_ref[...] += jnp.dot(act, wd_ref[0], preferred_element_type=_F32)

    last = j == pl.num_programs(1) - 1

    @pl.when(active & last)
    def _():
        o_ref[...] = acc_ref[...]

    @pl.when(jnp.logical_not(active) & last)
    def _():
        o_ref[...] = jnp.zeros(o_ref.shape, _F32)


def _experts(x, gain, w_gu, w_down, row_tok, block_e, n_used, *, bm, tf=1792):
    n, d = x.shape
    f = w_down.shape[1]
    tf = _tile(f, tf)
    nf = f // tf
    n_blocks = block_e.shape[0]
    assert bm % nf == 0

    def col(i, j, nu):
        return jnp.where(i < nu[0], j, nf - 1)

    grid_spec = pltpu.PrefetchScalarGridSpec(
        num_scalar_prefetch=2,
        grid=(n_blocks, nf),
        in_specs=[
            pl.BlockSpec((1, 1, bm), lambda i, j, be, nu: (i, 0, 0), memory_space=pltpu.SMEM),
            pl.BlockSpec((1, 1, bm), lambda i, j, be, nu: (jnp.minimum(i + 1, n_blocks - 1), 0, 0),
                         memory_space=pltpu.SMEM),
            pl.BlockSpec(memory_space=pl.ANY),
            pl.BlockSpec((1, d), lambda i, j, be, nu: (0, 0)),
            pl.BlockSpec((1, d, tf), lambda i, j, be, nu: (be[i], 0, col(i, j, nu))),
            pl.BlockSpec((1, d, tf), lambda i, j, be, nu: (be[i], 0, nf + col(i, j, nu))),
            pl.BlockSpec((1, tf, d), lambda i, j, be, nu: (be[i], col(i, j, nu), 0)),
        ],
        out_specs=pl.BlockSpec((bm, d), lambda i, j, be, nu: (i, 0)),
        scratch_shapes=[
            pltpu.VMEM((2, bm, d), _F32),
            pltpu.VMEM((bm, d), _BF16),
            pltpu.VMEM((bm, d), _F32),
            pltpu.SemaphoreType.DMA((2,)),
        ],
    )
    row_tok_b = row_tok.reshape(n_blocks, 1, bm)
    return pl.pallas_call(
        functools.partial(_expert_kernel, rows_per_step=bm // nf),
        grid_spec=grid_spec,
        out_shape=jax.ShapeDtypeStruct((n_blocks * bm, d), _F32),
        compiler_params=_params("arbitrary", "arbitrary"),
        name="moe_experts",
    )(block_e, n_used, row_tok_b, row_tok_b, x, gain.reshape(1, d), w_gu, w_gu, w_down)


def _combine_kernel(d_ref, d_next_ref, x_ref, w_ref, y_hbm, gf_ref, o_ref, r_ref, sem):
    i = pl.program_id(0)
    slot = lax.rem(i, 2)

    def start(dref, s):
        for kk in range(_TOP_K):
            _start_row_gather(y_hbm, r_ref.at[s, kk], lambda r, kk=kk: dref[0, kk, r], sem.at[s],
                              static_rows=isinstance(s, int))

    @pl.when(i == 0)
    def _():
        start(d_ref, 0)

    for s_next in range(2):
        @pl.when((i + 1 < pl.num_programs(0)) & (slot == 1 - s_next))
        def _(s_next=s_next):
            start(d_next_ref, s_next)

    for kk in range(_TOP_K):
        _wait_row_gather(y_hbm, r_ref.at[slot, kk], sem.at[slot])
    w = w_ref[...]
    x = x_ref[...]
    for kk in range(_TOP_K):
        x = x + w[:, kk:kk + 1] * r_ref[slot, kk]
    o_ref[...] = _rms(x) * gf_ref[...]


def _combine_final(x, dest, wgt, yb, gain_final, *, tm=256):
    n, d = x.shape
    tm = _tile(n, tm)
    nt = n // tm
    dest_b = dest.reshape(nt, tm, _TOP_K).transpose(0, 2, 1)
    return pl.pallas_call(
        _combine_kernel,
        grid=(nt,),
        in_specs=[
            pl.BlockSpec((1, _TOP_K, tm), lambda i: (i, 0, 0), memory_space=pltpu.SMEM),
            pl.BlockSpec((1, _TOP_K, tm), lambda i: (jnp.minimum(i + 1, nt - 1), 0, 0), memory_space=pltpu.SMEM),
            pl.BlockSpec((tm, d), lambda i: (i, 0)),
            pl.BlockSpec((tm, _TOP_K), lambda i: (i, 0)),
            pl.BlockSpec(memory_space=pl.ANY),
            pl.BlockSpec((1, d), lambda i: (0, 0)),
        ],
        out_specs=pl.BlockSpec((tm, d), lambda i: (i, 0)),
        out_shape=jax.ShapeDtypeStruct((n, d), _F32),
        scratch_shapes=[
            pltpu.VMEM((2, _TOP_K, tm, d), _F32),
            pltpu.SemaphoreType.DMA((2,)),
        ],
        compiler_params=_params("arbitrary"),
        name="moe_combine_final_norm",
    )(dest_b, dest_b, x, wgt, yb, gain_final.reshape(1, d))


def _moe_final(x, gain, w_router, w_gu, w_down, gain_final, *, bm):
    n, d = x.shape
    ne = w_router.shape[1]
    idx, wgt = _router(x, gain, w_router.T)
    a = n * _TOP_K
    e_flat = idx.T.reshape(a)
    onehot = (e_flat[:, None] == jnp.arange(ne, dtype=jnp.int32)[None, :]).astype(jnp.int32)
    csum = jnp.cumsum(onehot, axis=0)
    rank = jnp.sum(csum * onehot, axis=1) - 1
    counts = csum[-1]
    padded = (counts + bm - 1) // bm * bm
    pad_end = jnp.cumsum(padded)
    pad_start = pad_end - padded
    dest = pad_start[e_flat] + rank
    assert a % bm == 0
    n_blocks = a // bm + ne
    row_tok = jnp.zeros((n_blocks * bm,), jnp.int32).at[dest].set(jnp.arange(a, dtype=jnp.int32) // _TOP_K)
    block_start = jnp.arange(n_blocks, dtype=jnp.int32) * bm
    block_e = jnp.minimum(jnp.sum((block_start[:, None] >= pad_end[None, :]).astype(jnp.int32), axis=1), ne - 1)
    n_used = (pad_end[-1:] // bm).astype(jnp.int32)
    yb = _experts(x, gain, w_gu, w_down, row_tok, block_e, n_used, bm=bm)
    return _combine_final(x, dest.reshape(n, _TOP_K), wgt.T, yb, gain_final)


def _rope_tables(pos, dim, lanes_signed):
    half = dim // 2
    inv = _ROPE_THETA ** (-jnp.arange(half, dtype=_F32) / half)
    ang = pos.astype(_F32)[:, None] * inv[None, :]
    cos, sin = jnp.cos(ang), jnp.sin(ang)
    if not lanes_signed:
        return cos, sin
    reps = _LANES // dim
    cos_l = jnp.tile(jnp.concatenate([cos, cos], axis=1), (1, reps))
    sin_l = jnp.tile(jnp.concatenate([-sin, sin], axis=1), (1, reps))
    return cos_l, sin_l


def _trunk(x, pos, p0, ret_s0, past_k, past_v, mem_k, mem_v, w, *, big):
    b, t, d = x.shape
    n = b * t
    xf = x.reshape(n, d)
    tm = 512 if big else 256
    dk = d // _RET_HEADS
    hd_all = d

    cos, sin = _rope_tables(pos, dk, False)
    if cos.shape[0] % tm != 0:
        cos, sin = jnp.tile(cos, (b, 1)), jnp.tile(sin, (b, 1))
    (qkvg,) = _proj(xf, w["norm_mix"][0], w["ret_w_in"], cos, sin, outs=[(_BF16, 0, w["ret_w_in"].shape[1])],
                    half=dk // 2, n_rope_cols=2 * d, n_first_cols=d, scale_first=1.0, scale_rest=dk ** -0.5,
                    tm=2 * tm, tn=1024)
    o, ret_state = _retention(qkvg, ret_s0, b, t)
    xf = _out_proj(o, w["ret_w_out"], xf, tm=tm)
    xf = _cross_attention(xf, w["norm_cross"][0], w["cross_wq"][0], w["cross_wo"][0],
                          mem_k[0].reshape(b, -1, d), mem_v[0].reshape(b, -1, d), b, t, tm=tm)
    xf = _dense_ffn(xf, w["norm_ffn"][0], w["ffn_w_gu"], w["ffn_w_down"], tm=tm)

    lam_init = 0.8 - 0.6 * math.exp(-0.3 * 1)
    cos, sin = _rope_tables(pos, _DIFF_HD, True)
    if cos.shape[0] % tm != 0:
        cos, sin = jnp.tile(cos, (b, 1)), jnp.tile(sin, (b, 1))
    qkv, k_rows, v_rows = _proj(
        xf, w["norm_mix"][1], w["diff_w_in"], cos, sin,
        outs=[(_BF16, 0, 3 * hd_all), (_F32, hd_all, 2 * hd_all), (_F32, 2 * hd_all, 3 * hd_all)],
        half=_DIFF_HD // 2, n_rope_cols=2 * hd_all, n_first_cols=hd_all,
        scale_first=_DIFF_HD ** -0.5 * math.log2(math.e),
        scale_rest=1.0, tm=tm, tn=1024)
    n_heads = d // (2 * _DIFF_HD)
    if past_k is None:
        o = _diff_attention(qkv, qkv, qkv, 0, n_heads, 2 * n_heads, w["diff_lambda"], w["diff_subln"], b, t, p0,
                            lam_init, n_heads, tq=512, tk=512)
    else:
        n_past = past_k.shape[1]
        n_keys = n_past + t
        k_new = qkv[:, hd_all:2 * hd_all].reshape(b, t, d)
        v_new = qkv[:, 2 * hd_all:].reshape(b, t, d)
        k_all = jnp.concatenate([past_k.reshape(b, n_past, d).astype(_BF16), k_new], axis=1)
        v_all = jnp.concatenate([past_v.reshape(b, n_past, d).astype(_BF16), v_new], axis=1)
        o = _diff_attention(qkv, k_all.reshape(b * n_keys, d), v_all.reshape(b * n_keys, d), 0, 0, 0,
                            w["diff_lambda"], w["diff_subln"], b, t, p0, lam_init, n_heads, tq=t, tk=512)
    xf = _out_proj(o, w["diff_w_out"], xf, tm=tm)
    xf = _cross_attention(xf, w["norm_cross"][1], w["cross_wq"][1], w["cross_wo"][1],
                          mem_k[1].reshape(b, -1, d), mem_v[1].reshape(b, -1, d), b, t, tm=tm)
    y = _moe_final(xf, w["norm_ffn"][1], w["moe_router"], w["moe_w_gu"], w["moe_w_down"], w["norm_final"],
                   bm=512 if big else 128)
    hw = 2 * _DIFF_HD
    return (y.reshape(b, t, d), ret_state, k_rows.reshape(b, t, n_heads, hw), v_rows.reshape(b, t, n_heads, hw))


def kernel(x_prompt, x_sample, state_ret, cache_diff_k, cache_diff_v, cache_mem_k, cache_mem_v, mem_prompt, norm_mix,
           norm_cross, norm_mem, norm_ffn, norm_final, ret_w_in, ret_w_out, diff_w_in, diff_lambda, diff_subln,
           diff_w_out, cross_wq, cross_wkv, cross_wo, ffn_w_gu, ffn_w_down, moe_router, moe_w_gu, moe_w_down):
    depth = norm_mix.shape[0]
    assert depth == 2, "one retention layer followed by one differential-attention layer"
    bf = lambda a: a.astype(_BF16)
    w = dict(norm_mix=norm_mix, norm_cross=norm_cross, norm_ffn=norm_ffn, norm_final=norm_final,
             ret_w_in=bf(ret_w_in[0]), ret_w_out=bf(ret_w_out[0]), diff_w_in=bf(diff_w_in[0]),
             diff_lambda=diff_lambda[0], diff_subln=diff_subln[0], diff_w_out=bf(diff_w_out[0]),
             cross_wq=bf(cross_wq), cross_wo=bf(cross_wo), ffn_w_gu=bf(ffn_w_gu[0]), ffn_w_down=bf(ffn_w_down[0]),
             moe_router=moe_router[0], moe_w_gu=bf(moe_w_gu[0]), moe_w_down=bf(moe_w_down[0]))

    b_p, t_p, d = x_prompt.shape
    b_s, t_s, _ = x_sample.shape
    mem_len = mem_prompt.shape[1]
    mem_flat = mem_prompt.reshape(b_p * mem_len, d)
    mks, mvs = [], []
    for i in range(depth):
        mk, mv = _proj(mem_flat, norm_mem[i], bf(cross_wkv[i]), None, None, outs=[(_F32, 0, d), (_F32, d, 2 * d)],
                       tm=256)
        mks.append(mk.reshape(b_p, mem_len, _CROSS_HEADS, d // _CROSS_HEADS))
        mvs.append(mv.reshape(b_p, mem_len, _CROSS_HEADS, d // _CROSS_HEADS))
    mem_k_p, mem_v_p = jnp.stack(mks), jnp.stack(mvs)

    dkr = d // _RET_HEADS
    s0 = jnp.zeros((b_p, _RET_HEADS, dkr, 2 * dkr), _F32)
    y_p, rs_p, dk_p, dv_p = _trunk(x_prompt, jnp.arange(t_p, dtype=jnp.int32), 0, s0, None, None, mem_k_p, mem_v_p, w,
                                   big=True)
    n_past = cache_diff_k.shape[2]
    y_s, rs_s, dk_s, dv_s = _trunk(x_sample, n_past + jnp.arange(t_s, dtype=jnp.int32), n_past, state_ret[0],
                                   cache_diff_k[0], cache_diff_v[0], cache_mem_k, cache_mem_v, w, big=False)
    return (y_p, y_s, rs_p[None], dk_p[None], dv_p[None], mem_k_p, mem_v_p, rs_s[None], dk_s[None], dv_s[None])
```

```python
import functools
import math

import jax
import jax.numpy as jnp
from jax import lax
from jax.experimental import pallas as pl
from jax.experimental.pallas import tpu as pltpu

_CHUNK = 64
_ROPE_THETA = 10000.0
_EPS = 1e-6
_RET_HEADS = 4
_DIFF_HD = 64
_CROSS_HEADS = 4
_N_EXPERTS = 8
_TOP_K = 2

_VMEM_LIMIT_V7X = 56 * 1024 * 1024
_LANES = 128

_F32 = jnp.float32
_BF16 = jnp.bfloat16
_NT = (((1,), (1,)), ((), ()))
_TN = (((0,), (0,)), ((), ()))
_NEG = -1e30


def _params(*sem):
    return pltpu.CompilerParams(dimension_semantics=sem, vmem_limit_bytes=_VMEM_LIMIT_V7X)


def _tile(n, pref):
    t = min(n, pref)
    assert n % t == 0, (n, pref)
    return t


def _rms(x):
    return x * lax.rsqrt(jnp.mean(x * x, axis=-1, keepdims=True) + _EPS)


def _silu(x):
    return x * jax.nn.sigmoid(x)


def _proj_kernel(x_ref, g_ref, w_ref, cos_ref, sin_ref, rot_ref, *rest, half, n_rope, n_first, scale_first,
                 scale_rest, out_ranges):
    o_refs, xn_ref = rest[:-1], rest[-1]
    j = pl.program_id(1)

    @pl.when(j == 0)
    def _():
        xn_ref[...] = (_rms(x_ref[...]) * g_ref[...]).astype(_BF16)

    acc = jnp.dot(xn_ref[...], w_ref[...], preferred_element_type=_F32)
    tn = acc.shape[1]

    def emit(val, lo, hi):
        for o_ref, (olo, ohi) in zip(o_refs, out_ranges):
            if olo >= hi or ohi <= lo:
                continue
            if olo <= lo and hi <= ohi:
                o_ref[...] = val.astype(o_ref.dtype)
            else:
                @pl.when((j >= olo) & (j < ohi))
                def _(o_ref=o_ref):
                    o_ref[...] = val.astype(o_ref.dtype)

    if n_rope > 0:
        @pl.when(j < n_rope)
        def _():
            cos = cos_ref[...]
            sin = sin_ref[...]
            pieces = []
            if half == _LANES:
                for c in range(tn // (2 * _LANES)):
                    x1 = acc[:, 2 * c * _LANES:(2 * c + 1) * _LANES]
                    x2 = acc[:, (2 * c + 1) * _LANES:(2 * c + 2) * _LANES]
                    pieces += [x1 * cos - x2 * sin, x2 * cos + x1 * sin]
            else:
                wr = rot_ref.shape[0]
                cos_w = jnp.concatenate([cos] * (wr // _LANES), axis=1)
                sin_w = jnp.concatenate([sin] * (wr // _LANES), axis=1)
                for c in range(tn // wr):
                    xc = acc[:, c * wr:(c + 1) * wr]
                    rot = jnp.dot(xc.astype(_BF16), rot_ref[...], preferred_element_type=_F32)
                    pieces.append(xc * cos_w + rot * sin_w)
            r = jnp.concatenate(pieces, axis=1)
            scale = jnp.where(j < n_first, scale_first, scale_rest).astype(_F32)
            emit(r * scale, 0, n_rope)

        @pl.when(j >= n_rope)
        def _():
            emit(acc, n_rope, 1 << 30)
    else:
        emit(acc, 0, 1 << 30)


def _proj(x, gain, w, cos, sin, *, outs, half=_LANES, n_rope_cols=0, n_first_cols=0, scale_first=1.0,
          scale_rest=1.0, tm=512, tn=512):
    n, d = x.shape
    d_out = w.shape[1]
    tm = _tile(n, tm)
    tn = _tile(d_out, tn)
    assert n_rope_cols % tn == 0 and n_first_cols % tn == 0
    if cos is None:
        cos = jnp.zeros((tm, _LANES), _F32)
        sin = cos
    r_tiles = cos.shape[0] // tm
    assert cos.shape[0] % tm == 0
    if n_rope_cols > 0 and half != _LANES:
        wr = 2 * _LANES
        dst = jnp.arange(wr)
        partner = jnp.where(dst % (2 * half) < half, dst + half, dst - half)
        rot = (jnp.arange(wr)[:, None] == partner[None, :]).astype(_BF16)
    else:
        rot = jnp.zeros((8, _LANES), _BF16)
    out_ranges = []
    for _, lo, hi in outs:
        assert lo % tn == 0 and hi % tn == 0
        out_ranges.append((lo // tn, hi // tn))
    kern = functools.partial(_proj_kernel, half=half, n_rope=n_rope_cols // tn, n_first=n_first_cols // tn,
                             scale_first=scale_first, scale_rest=scale_rest, out_ranges=tuple(out_ranges))

    def out_map(lo, hi):
        return lambda i, j: (i, jnp.clip(j - lo, 0, hi - lo - 1))

    return pl.pallas_call(
        kern,
        grid=(n // tm, d_out // tn),
        in_specs=[
            pl.BlockSpec((tm, d), lambda i, j: (i, 0)),
            pl.BlockSpec((1, d), lambda i, j: (0, 0)),
            pl.BlockSpec((d, tn), lambda i, j: (0, j)),
            pl.BlockSpec((tm, _LANES), lambda i, j: (i % r_tiles, 0)),
            pl.BlockSpec((tm, _LANES), lambda i, j: (i % r_tiles, 0)),
            pl.BlockSpec(rot.shape, lambda i, j: (0, 0)),
        ],
        out_specs=[pl.BlockSpec((tm, tn), out_map(lo, hi)) for lo, hi in out_ranges],
        out_shape=[jax.ShapeDtypeStruct((n, hi - lo), dt) for dt, lo, hi in outs],
        scratch_shapes=[pltpu.VMEM((tm, d), _BF16)],
        compiler_params=_params("parallel", "arbitrary"),
        name="norm_proj",
    )(x, gain.reshape(1, d), w, cos, sin, rot)


def _out_proj_kernel(a_ref, w_ref, r_ref, o_ref):
    o_ref[...] = r_ref[...] + jnp.dot(a_ref[...], w_ref[...], preferred_element_type=_F32)


def _out_proj(a, w, res, *, tm=512):
    n, k = a.shape
    d = w.shape[1]
    tm = _tile(n, tm)
    return pl.pallas_call(
        _out_proj_kernel,
        grid=(n // tm,),
        in_specs=[
            pl.BlockSpec((tm, k), lambda i: (i, 0)),
            pl.BlockSpec((k, d), lambda i: (0, 0)),
            pl.BlockSpec((tm, d), lambda i: (i, 0)),
        ],
        out_specs=pl.BlockSpec((tm, d), lambda i: (i, 0)),
        out_shape=jax.ShapeDtypeStruct((n, d), _F32),
        compiler_params=_params("parallel"),
        name="out_proj_residual",
    )(a, w, res)


def _retention_kernel(q_ref, k_ref, v_ref, g_ref, s0_ref, dm_ref, qd_ref, kd_ref, cd_ref, x_ref, wo_ref, xo_ref, so_ref,
                      s_ref, og_ref):
    n = pl.program_id(1)
    heads, dk, dv = s_ref.shape

    @pl.when(n == 0)
    def _():
        s_ref[...] = s0_ref[0]

    for hh in range(heads):
        q = q_ref[:, hh * dk:(hh + 1) * dk]
        k = k_ref[:, hh * dk:(hh + 1) * dk]
        v = v_ref[:, hh * dv:(hh + 1) * dv]
        sc = lax.dot_general(q, k, _NT, preferred_element_type=_F32)
        p = (sc * dm_ref[hh]).astype(_BF16)
        s_old = s_ref[hh]
        q_dec = (q.astype(_F32) * qd_ref[hh]).astype(_BF16)
        o = jnp.dot(p, v, preferred_element_type=_F32) + jnp.dot(q_dec, s_old.astype(_BF16),
                                                                preferred_element_type=_F32)
        k_dec = (k.astype(_F32) * kd_ref[hh]).astype(_BF16)
        s_ref[hh] = s_old * cd_ref[hh] + lax.dot_general(k_dec, v, _TN, preferred_element_type=_F32)
        g = g_ref[:, hh * dv:(hh + 1) * dv].astype(_F32)
        og_ref[:, hh * dv:(hh + 1) * dv] = (_silu(g) * _rms(o)).astype(og_ref.dtype)

    xo_ref[...] = x_ref[...] + jnp.dot(og_ref[...], wo_ref[...], preferred_element_type=_F32)

    @pl.when(n == pl.num_programs(1) - 1)
    def _():
        so_ref[0] = s_ref[...]


def _retention(qkvg, s0, x, w_out, batch, t, *, block=256):
    d = x.shape[1]
    h = _RET_HEADS
    dk, dv = s0.shape[2], s0.shape[3]
    lb = _tile(t, block)
    nb = t // lb
    assert lb % _CHUNK == 0
    log_g = jnp.log1p(-(2.0 ** (-5.0 - jnp.arange(h, dtype=_F32))))
    i = jnp.arange(lb, dtype=_F32)
    ci = jnp.arange(lb) // _CHUNK
    diff = i[:, None] - i[None, :]
    same = ci[:, None] == ci[None, :]
    before = ci[None, :] < ci[:, None]
    expo = jnp.where(same, jnp.abs(diff), diff)
    dmask = jnp.where((same | before)[None], jnp.exp(expo[None] * log_g[:, None, None]), 0.0).astype(_F32)
    q_dec = jnp.exp((i[None, :] + 1.0) * log_g[:, None])[..., None]
    k_dec = jnp.exp((lb - 1.0 - i)[None, :] * log_g[:, None])[..., None]
    c_dec = jnp.exp(lb * log_g).reshape(h, 1, 1)
    wq, wv = h * dk, h * dv
    assert (2 * wq) % wv == 0
    v_blk = (2 * wq) // wv
    return pl.pallas_call(
        _retention_kernel,
        grid=(batch, nb),
        in_specs=[
            pl.BlockSpec((lb, wq), lambda b, n: (b * nb + n, 0)),
            pl.BlockSpec((lb, wq), lambda b, n: (b * nb + n, 1)),
            pl.BlockSpec((lb, wv), lambda b, n: (b * nb + n, v_blk)),
            pl.BlockSpec((lb, wv), lambda b, n: (b * nb + n, v_blk + 1)),
            pl.BlockSpec((1, h, dk, dv), lambda b, n: (b, 0, 0, 0)),
            pl.BlockSpec((h, lb, lb), lambda b, n: (0, 0, 0)),
            pl.BlockSpec((h, lb, 1), lambda b, n: (0, 0, 0)),
            pl.BlockSpec((h, lb, 1), lambda b, n: (0, 0, 0)),
            pl.BlockSpec((h, 1, 1), lambda b, n: (0, 0, 0)),
            pl.BlockSpec((lb, d), lambda b, n: (b * nb + n, 0)),
            pl.BlockSpec((wv, d), lambda b, n: (0, 0)),
        ],
        out_specs=[
            pl.BlockSpec((lb, d), lambda b, n: (b * nb + n, 0)),
            pl.BlockSpec((1, h, dk, dv), lambda b, n: (b, 0, 0, 0)),
        ],
        out_shape=[
            jax.ShapeDtypeStruct((batch * t, d), _F32),
            jax.ShapeDtypeStruct((batch, h, dk, dv), _F32),
        ],
        scratch_shapes=[pltpu.VMEM((h, dk, dv), _F32), pltpu.VMEM((lb, wv), _BF16)],
        compiler_params=_params("parallel", "arbitrary"),
        name="retention",
    )(qkvg, qkvg, qkvg, qkvg, s0, dmask, q_dec, k_dec, c_dec, x, w_out)


def _cross_kernel(x_ref, g_ref, wq_ref, wo_ref, mk_ref, mv_ref, o_ref, *, heads):
    x = x_ref[...]
    d = x.shape[1]
    hd = d // heads
    xn = (_rms(x) * g_ref[...]).astype(_BF16)
    q = (jnp.dot(xn, wq_ref[...], preferred_element_type=_F32) * (hd ** -0.5)).astype(_BF16)
    mk = mk_ref[0].astype(_BF16)
    mv = mv_ref[0].astype(_BF16)
    outs = []
    for hh in range(heads):
        sl = slice(hh * hd, (hh + 1) * hd)
        s = lax.dot_general(q[:, sl], mk[:, sl], _NT, preferred_element_type=_F32)
        s = s - jnp.max(s, axis=-1, keepdims=True)
        e = jnp.exp(s)
        p = (e / jnp.sum(e, axis=-1, keepdims=True)).astype(_BF16)
        outs.append(jnp.dot(p, mv[:, sl], preferred_element_type=_F32).astype(_BF16))
    o = jnp.concatenate(outs, axis=1)
    o_ref[...] = x + jnp.dot(o, wo_ref[...], preferred_element_type=_F32)


def _cross_attention(x, gain, wq, wo, mk, mv, batch, t, *, tm=512):
    n, d = x.shape
    tm = _tile(t, tm)
    nt = t // tm
    mem = mk.shape[1]
    return pl.pallas_call(
        functools.partial(_cross_kernel, heads=_CROSS_HEADS),
        grid=(batch, nt),
        in_specs=[
            pl.BlockSpec((tm, d), lambda b, i: (b * nt + i, 0)),
            pl.BlockSpec((1, d), lambda b, i: (0, 0)),
            pl.BlockSpec((d, d), lambda b, i: (0, 0)),
            pl.BlockSpec((d, d), lambda b, i: (0, 0)),
            pl.BlockSpec((1, mem, d), lambda b, i: (b, 0, 0)),
            pl.BlockSpec((1, mem, d), lambda b, i: (b, 0, 0)),
        ],
        out_specs=pl.BlockSpec((tm, d), lambda b, i: (b * nt + i, 0)),
        out_shape=jax.ShapeDtypeStruct((n, d), _F32),
        compiler_params=_params("parallel", "parallel"),
        name="cross_attention",
    )(x, gain.reshape(1, d), wq, wo, mk, mv)


def _ffn_kernel(x_ref, g_ref, wg_ref, wu_ref, wd_ref, o_ref, xn_ref, acc_ref):
    j = pl.program_id(1)

    @pl.when(j == 0)
    def _():
        xn_ref[...] = (_rms(x_ref[...]) * g_ref[...]).astype(_BF16)
        acc_ref[...] = x_ref[...]

    xn = xn_ref[...]
    gate = jnp.dot(xn, wg_ref[...], preferred_element_type=_F32)
    up = jnp.dot(xn, wu_ref[...], preferred_element_type=_F32)
    act = (_silu(gate) * up).astype(_BF16)
    acc_ref[...] += jnp.dot(act, wd_ref[...], preferred_element_type=_F32)

    @pl.when(j == pl.num_programs(1) - 1)
    def _():
        o_ref[...] = acc_ref[...]


def _dense_ffn(x, gain, w_gu, w_down, *, tm=512, tf=1408):
    n, d = x.shape
    f = w_down.shape[0]
    tm = _tile(n, tm)
    tf = _tile(f, tf)
    nf = f // tf
    return pl.pallas_call(
        _ffn_kernel,
        grid=(n // tm, nf),
        in_specs=[
            pl.BlockSpec((tm, d), lambda i, j: (i, 0)),
            pl.BlockSpec((1, d), lambda i, j: (0, 0)),
            pl.BlockSpec((d, tf), lambda i, j: (0, j)),
            pl.BlockSpec((d, tf), lambda i, j: (0, nf + j)),
            pl.BlockSpec((tf, d), lambda i, j: (j, 0)),
        ],
        out_specs=pl.BlockSpec((tm, d), lambda i, j: (i, 0)),
        out_shape=jax.ShapeDtypeStruct((n, d), _F32),
        scratch_shapes=[pltpu.VMEM((tm, d), _BF16), pltpu.VMEM((tm, d), _F32)],
        compiler_params=_params("parallel", "arbitrary"),
        name="dense_swiglu",
    )(x, gain.reshape(1, d), w_gu, w_gu, w_down)


def _diff_attn_kernel(q_ref, k_ref, v_ref, lam_ref, sub_ref, o_ref, q2_ref, m_ref, acc_ref, s_ref, *, p0, tq, tk,
                      lam_init):
    qi = pl.program_id(2)
    hd = _DIFF_HD
    hw = 2 * hd

    q = q_ref[...]
    first = lax.broadcasted_iota(jnp.int32, (1, hw), 1) < hd
    q2_ref[0:tq, :] = jnp.where(first, q, 0.0).astype(_BF16)
    q2_ref[tq:2 * tq, :] = jnp.where(first, 0.0, q).astype(_BF16)
    m_ref[...] = jnp.full(m_ref.shape, _NEG, _F32)
    acc_ref[...] = jnp.zeros(acc_ref.shape, _F32)

    def scores(off, nk):
        return lax.dot_general(q2_ref[...], k_ref[pl.ds(off, nk), :], _NT, preferred_element_type=_F32)

    def softmax_pv(s, off, nk, visible):
        v_blk = v_ref[pl.ds(off, nk), :]
        v_ext = jnp.concatenate([v_blk, jnp.ones(v_blk.shape, v_blk.dtype)], axis=1)
        if visible is not None:
            s = jnp.where(visible, s, _NEG)
        m_old = m_ref[...]
        m_new = jnp.maximum(m_old, jnp.max(s, axis=-1, keepdims=True))
        alpha = jnp.exp2(m_old - m_new)
        if nk % _LANES == 0:
            p = jnp.exp2(s - jnp.concatenate([m_new] * (nk // _LANES), axis=1))
        else:
            p = jnp.exp2(s - m_new[:, 0:1])
        pv = jnp.dot(p.astype(_BF16), v_ext, preferred_element_type=_F32)
        acc_ref[...] = jnp.concatenate([alpha, alpha], axis=1) * acc_ref[...] + pv
        m_ref[...] = m_new

    q_start = p0 + qi * tq
    n_full = q_start // tk
    r_chunk = lax.broadcasted_iota(jnp.int32, (2 * tq, tq), 0) % tq // _CHUNK
    c_chunk = lax.broadcasted_iota(jnp.int32, (2 * tq, tq), 1) // _CHUNK
    vis = c_chunk <= r_chunk
    diag_off = pl.multiple_of(q_start, tq)

    if tq == tk:
        s_ref[0] = scores(0, tk)

        def pair(j):
            off0 = pl.multiple_of(2 * j * tk, tk)
            off1 = pl.multiple_of(off0 + tk, tk)
            s_ref[1] = scores(off1, tk)
            softmax_pv(s_ref[0], off0, tk, None)
            s_ref[0] = scores(pl.multiple_of(off1 + tk, tk), tk)
            softmax_pv(s_ref[1], off1, tk, None)

        n_pairs = n_full // 2
        group = _ATTN_PAIRS_PER_TRIP

        def body(g, carry):
            for u in range(group):
                pair(g * group + u)
            return carry

        lax.fori_loop(0, n_pairs // group, body, 0)
        done = n_pairs // group * group
        width = group // 2
        while width >= 1:
            @pl.when((n_pairs - done) % (2 * width) >= width)
            def _(done=done, width=width):
                for u in range(width):
                    pair(done + u)
            done = done + jnp.where((n_pairs - done) % (2 * width) >= width, width, 0)
            width //= 2

        @pl.when(n_full % 2 == 1)
        def _():
            s_ref[1] = scores(diag_off, tk)
            softmax_pv(s_ref[0], pl.multiple_of(diag_off - tk, tk), tk, None)
            softmax_pv(s_ref[1], diag_off, tq, vis)

        @pl.when(n_full % 2 == 0)
        def _():
            softmax_pv(s_ref[0], diag_off, tq, vis)
    else:
        def body(i, carry):
            off = pl.multiple_of(i * tk, tk)
            softmax_pv(scores(off, tk), off, tk, None)
            return carry

        lax.fori_loop(0, n_full, body, 0)
        softmax_pv(scores(diag_off, tq), diag_off, tq, vis)


    lp = lam_ref[...]
    lam = (jnp.exp(jnp.sum(lp[0:1] * lp[1:2], axis=-1, keepdims=True))
           - jnp.exp(jnp.sum(lp[2:3] * lp[3:4], axis=-1, keepdims=True)) + lam_init)
    acc = acc_ref[...]
    o = acc[0:tq, 0:hw] / acc[0:tq, hw:2 * hw] - lam * (acc[tq:2 * tq, 0:hw] / acc[tq:2 * tq, hw:2 * hw])
    o_ref[...] = (_rms(o) * sub_ref[...] * (1.0 - lam_init)).astype(o_ref.dtype)


def _diff_attention(q_arr, k_arr, v_arr, q_col, k_col, v_col, lam_p, subln, batch, t, p0, lam_init, n_heads, *, tq,
                    tk):
    hw = 2 * _DIFF_HD
    n_keys = p0 + t
    tq = _tile(t, tq)
    nq = t // tq
    tk = math.gcd(tk, p0, tq) if nq > 1 else math.gcd(tk, p0)
    assert tq % _CHUNK == 0 and tk % _CHUNK == 0
    kern = functools.partial(_diff_attn_kernel, p0=p0, tq=tq, tk=tk, lam_init=lam_init)
    return pl.pallas_call(
        kern,
        grid=(batch, n_heads, nq),
        in_specs=[
            pl.BlockSpec((tq, hw), lambda b, hh, qi: (b * nq + qi, q_col + hh)),
            pl.BlockSpec((n_keys, hw), lambda b, hh, qi: (b, k_col + hh)),
            pl.BlockSpec((n_keys, hw), lambda b, hh, qi: (b, v_col + hh)),
            pl.BlockSpec(lam_p.shape, lambda b, hh, qi: (0, 0)),
            pl.BlockSpec((1, hw), lambda b, hh, qi: (0, 0)),
        ],
        out_specs=pl.BlockSpec((tq, hw), lambda b, hh, qi: (b * nq + qi, hh)),
        out_shape=jax.ShapeDtypeStruct((batch * t, n_heads * hw), _BF16),
        scratch_shapes=[
            pltpu.VMEM((2 * tq, hw), _BF16),
            pltpu.VMEM((2 * tq, _LANES), _F32),
            pltpu.VMEM((2 * tq, 2 * hw), _F32),
            pltpu.VMEM((2, 2 * tq, tk) if tq == tk else (2, 8, _LANES), _F32),
        ],
        compiler_params=_params("parallel", "parallel", "arbitrary"),
        name="diff_attention",
    )(q_arr, k_arr, v_arr, lam_p, subln.reshape(1, hw))


def _router_kernel(x_ref, g_ref, wr_ref, idx_ref, wgt_ref):
    xn = _rms(x_ref[...]) * g_ref[...]
    logits = lax.dot_general(wr_ref[...], xn, _NT, preferred_element_type=_F32, precision=lax.Precision.HIGHEST)
    ne = logits.shape[0]
    row = lax.broadcasted_iota(jnp.int32, logits.shape, 0)
    v1 = jnp.max(logits, axis=0, keepdims=True)
    i1 = jnp.min(jnp.where(logits == v1, row, ne), axis=0, keepdims=True)
    rest = jnp.where(row == i1, -jnp.inf, logits)
    v2 = jnp.max(rest, axis=0, keepdims=True)
    i2 = jnp.min(jnp.where(rest == v2, row, ne), axis=0, keepdims=True)
    e2 = jnp.exp(v2 - v1)
    w1 = 1.0 / (1.0 + e2)
    idx_ref[...] = jnp.concatenate([i1, i2], axis=0)
    wgt_ref[...] = jnp.concatenate([w1, e2 * w1], axis=0)


def _router(x, gain, w_router_t, *, tm=512):
    n, d = x.shape
    ne = w_router_t.shape[0]
    tm = _tile(n, tm)
    return pl.pallas_call(
        _router_kernel,
        grid=(n // tm,),
        in_specs=[
            pl.BlockSpec((tm, d), lambda i: (i, 0)),
            pl.BlockSpec((1, d), lambda i: (0, 0)),
            pl.BlockSpec((ne, d), lambda i: (0, 0)),
        ],
        out_specs=[
            pl.BlockSpec((_TOP_K, tm), lambda i: (0, i)),
            pl.BlockSpec((_TOP_K, tm), lambda i: (0, i)),
        ],
        out_shape=[
            jax.ShapeDtypeStruct((_TOP_K, n), jnp.int32),
            jax.ShapeDtypeStruct((_TOP_K, n), _F32),
        ],
        compiler_params=_params("parallel"),
        name="moe_router",
    )(x, gain.reshape(1, d), w_router_t)


_GATHER_UNROLL = 8
_ATTN_PAIRS_PER_TRIP = 4


def _start_row_gather(src_hbm, dst_vmem, idx_at, sem, *, static_rows=False):
    def start(r, c):
        pltpu.make_async_copy(src_hbm.at[pl.ds(idx_at(r), 1), :], dst_vmem.at[pl.ds(r, 1), :], sem).start()
        return c

    if static_rows:
        for r in range(dst_vmem.shape[0]):
            start(r, 0)
    else:
        lax.fori_loop(0, dst_vmem.shape[0], start, 0, unroll=_GATHER_UNROLL)


def _wait_row_gather(src_hbm, dst_vmem, sem):
    pltpu.make_async_copy(src_hbm.at[pl.ds(0, dst_vmem.shape[0]), :], dst_vmem, sem).wait()


def _expert_kernel(be_ref, nu_ref, rt_ref, rt_next_ref, x_hbm, g_ref, wg_ref, wu_ref, wd_ref, o_ref, xg_ref, xn_ref,
                   acc_ref, sem, *, rows_per_step):
    del be_ref
    i = pl.program_id(0)
    j = pl.program_id(1)
    n_used = nu_ref[0]
    active = i < n_used
    slot = lax.rem(i, 2)

    @pl.when(active & (i == 0) & (j == 0))
    def _():
        _start_row_gather(x_hbm, xg_ref.at[0], lambda r: rt_ref[0, 0, r], sem.at[0])

    @pl.when((i <= n_used) & (j == 0))
    def _():
        _wait_row_gather(x_hbm, xg_ref.at[slot], sem.at[slot])

    @pl.when(active & (j == 0))
    def _():
        xn_ref[...] = (_rms(xg_ref[slot]) * g_ref[...]).astype(_BF16)
        acc_ref[...] = jnp.zeros(acc_ref.shape, _F32)

    @pl.when(active)
    def _():
        base = j * rows_per_step
        for c in range(rows_per_step):
            r = base + c
            pltpu.make_async_copy(x_hbm.at[pl.ds(rt_next_ref[0, 0, r], 1), :],
                                  xg_ref.at[1 - slot, pl.ds(r, 1), :], sem.at[1 - slot]).start()
        xn = xn_ref[...]
        gate = jnp.dot(xn, wg_ref[0], preferred_element_type=_F32)
        up = jnp.dot(xn, wu_ref[0], preferred_element_type=_F32)
        act = (_silu(gate) * up).astype(_BF16)
        acc_ref[...] += jnp.dot(act, wd_ref[0], preferred_element_type=_F32)

    last = j == pl.num_programs(1) - 1

    @pl.when(active & last)
    def _():
        o_ref[...] = acc_ref[...]

    @pl.when(jnp.logical_not(active) & last)
    def _():
        o_ref[...] = jnp.zeros(o_ref.shape, _F32)


def _experts(x, gain, w_gu, w_down, row_tok, block_e, n_used, *, bm, tf=1792):
    n, d = x.shape
    f = w_down.shape[1]
    tf = _tile(f, tf)
    nf = f // tf
    n_blocks = block_e.shape[0]
    assert bm % nf == 0

    def col(i, j, nu):
        return jnp.where(i < nu[0], j, nf - 1)

    grid_spec = pltpu.PrefetchScalarGridSpec(
        num_scalar_prefetch=2,
        grid=(n_blocks, nf),
        in_specs=[
            pl.BlockSpec((1, 1, bm), lambda i, j, be, nu: (i, 0, 0), memory_space=pltpu.SMEM),
            pl.BlockSpec((1, 1, bm), lambda i, j, be, nu: (jnp.minimum(i + 1, n_blocks - 1), 0, 0),
                         memory_space=pltpu.SMEM),
            pl.BlockSpec(memory_space=pl.ANY),
            pl.BlockSpec((1, d), lambda i, j, be, nu: (0, 0)),
            pl.BlockSpec((1, d, tf), lambda i, j, be, nu: (be[i], 0, col(i, j, nu))),
            pl.BlockSpec((1, d, tf), lambda i, j, be, nu: (be[i], 0, nf + col(i, j, nu))),
            pl.BlockSpec((1, tf, d), lambda i, j, be, nu: (be[i], col(i, j, nu), 0)),
        ],
        out_specs=pl.BlockSpec((bm, d), lambda i, j, be, nu: (i, 0)),
        scratch_shapes=[
            pltpu.VMEM((2, bm, d), _F32),
            pltpu.VMEM((bm, d), _BF16),
            pltpu.VMEM((bm, d), _F32),
            pltpu.SemaphoreType.DMA((2,)),
        ],
    )
    row_tok_b = row_tok.reshape(n_blocks, 1, bm)
    return pl.pallas_call(
        functools.partial(_expert_kernel, rows_per_step=bm // nf),
        grid_spec=grid_spec,
        out_shape=jax.ShapeDtypeStruct((n_blocks * bm, d), _F32),
        compiler_params=_params("arbitrary", "arbitrary"),
        name="moe_experts",
    )(block_e, n_used, row_tok_b, row_tok_b, x, gain.reshape(1, d), w_gu, w_gu, w_down)


def _combine_kernel(d_ref, d_next_ref, x_ref, w_ref, y_hbm, gf_ref, o_ref, r_ref, sem):
    i = pl.program_id(0)
    slot = lax.rem(i, 2)

    def start(dref, s):
        for kk in range(_TOP_K):
            _start_row_gather(y_hbm, r_ref.at[s, kk], lambda r, kk=kk: dref[0, kk, r], sem.at[s],
                              static_rows=isinstance(s, int))

    @pl.when(i == 0)
    def _():
        start(d_ref, 0)

    for s_next in range(2):
        @pl.when((i + 1 < pl.num_programs(0)) & (slot == 1 - s_next))
        def _(s_next=s_next):
            start(d_next_ref, s_next)

    for kk in range(_TOP_K):
        _wait_row_gather(y_hbm, r_ref.at[slot, kk], sem.at[slot])
    w = w_ref[...]
    x = x_ref[...]
    for kk in range(_TOP_K):
        x = x + w[:, kk:kk + 1] * r_ref[slot, kk]
    o_ref[...] = _rms(x) * gf_ref[...]


def _combine_final(x, dest, wgt, yb, gain_final, *, tm=256):
    n, d = x.shape
    tm = _tile(n, tm)
    nt = n // tm
    dest_b = dest.reshape(nt, tm, _TOP_K).transpose(0, 2, 1)
    return pl.pallas_call(
        _combine_kernel,
        grid=(nt,),
        in_specs=[
            pl.BlockSpec((1, _TOP_K, tm), lambda i: (i, 0, 0), memory_space=pltpu.SMEM),
            pl.BlockSpec((1, _TOP_K, tm), lambda i: (jnp.minimum(i + 1, nt - 1), 0, 0), memory_space=pltpu.SMEM),
            pl.BlockSpec((tm, d), lambda i: (i, 0)),
            pl.BlockSpec((tm, _TOP_K), lambda i: (i, 0)),
            pl.BlockSpec(memory_space=pl.ANY),
            pl.BlockSpec((1, d), lambda i: (0, 0)),
        ],
        out_specs=pl.BlockSpec((tm, d), lambda i: (i, 0)),
        out_shape=jax.ShapeDtypeStruct((n, d), _F32),
        scratch_shapes=[
            pltpu.VMEM((2, _TOP_K, tm, d), _F32),
            pltpu.SemaphoreType.DMA((2,)),
        ],
        compiler_params=_params("arbitrary"),
        name="moe_combine_final_norm",
    )(dest_b, dest_b, x, wgt, yb, gain_final.reshape(1, d))


def _moe_final(x, gain, w_router, w_gu, w_down, gain_final, *, bm):
    n, d = x.shape
    ne = w_router.shape[1]
    idx, wgt = _router(x, gain, w_router.T)
    a = n * _TOP_K
    e_flat = idx.T.reshape(a)
    onehot = (e_flat[:, None] == jnp.arange(ne, dtype=jnp.int32)[None, :]).astype(jnp.int32)
    csum = jnp.cumsum(onehot, axis=0)
    rank = jnp.sum(csum * onehot, axis=1) - 1
    counts = csum[-1]
    padded = (counts + bm - 1) // bm * bm
    pad_end = jnp.cumsum(padded)
    pad_start = pad_end - padded
    dest = pad_start[e_flat] + rank
    assert a % bm == 0
    n_blocks = a // bm + ne
    row_tok = jnp.zeros((n_blocks * bm,), jnp.int32).at[dest].set(jnp.arange(a, dtype=jnp.int32) // _TOP_K)
    block_start = jnp.arange(n_blocks, dtype=jnp.int32) * bm
    block_e = jnp.minimum(jnp.sum((block_start[:, None] >= pad_end[None, :]).astype(jnp.int32), axis=1), ne - 1)
    n_used = (pad_end[-1:] // bm).astype(jnp.int32)
    yb = _experts(x, gain, w_gu, w_down, row_tok, block_e, n_used, bm=bm)
    return _combine_final(x, dest.reshape(n, _TOP_K), wgt.T, yb, gain_final)


def _rope_tables(pos, dim, lanes_signed):
    half = dim // 2
    inv = _ROPE_THETA ** (-jnp.arange(half, dtype=_F32) / half)
    ang = pos.astype(_F32)[:, None] * inv[None, :]
    cos, sin = jnp.cos(ang), jnp.sin(ang)
    if not lanes_signed:
        return cos, sin
    reps = _LANES // dim
    cos_l = jnp.tile(jnp.concatenate([cos, cos], axis=1), (1, reps))
    sin_l = jnp.tile(jnp.concatenate([-sin, sin], axis=1), (1, reps))
    return cos_l, sin_l


def _trunk(x, pos, p0, ret_s0, past_k, past_v, mem_k, mem_v, w, *, big):
    b, t, d = x.shape
    n = b * t
    xf = x.reshape(n, d)
    tm = 512 if big else 256
    dk = d // _RET_HEADS
    hd_all = d

    cos, sin = _rope_tables(pos, dk, False)
    if cos.shape[0] % tm != 0:
        cos, sin = jnp.tile(cos, (b, 1)), jnp.tile(sin, (b, 1))
    (qkvg,) = _proj(xf, w["norm_mix"][0], w["ret_w_in"], cos, sin, outs=[(_BF16, 0, w["ret_w_in"].shape[1])],
                    half=dk // 2, n_rope_cols=2 * d, n_first_cols=d, scale_first=1.0, scale_rest=dk ** -0.5,
                    tm=2 * tm, tn=1024)
    xf, ret_state = _retention(qkvg, ret_s0, xf, w["ret_w_out"], b, t)
    xf = _cross_attention(xf, w["norm_cross"][0], w["cross_wq"][0], w["cross_wo"][0],
                          mem_k[0].reshape(b, -1, d), mem_v[0].reshape(b, -1, d), b, t, tm=tm)
    xf = _dense_ffn(xf, w["norm_ffn"][0], w["ffn_w_gu"], w["ffn_w_down"], tm=tm)

    lam_init = 0.8 - 0.6 * math.exp(-0.3 * 1)
    cos, sin = _rope_tables(pos, _DIFF_HD, True)
    if cos.shape[0] % tm != 0:
        cos, sin = jnp.tile(cos, (b, 1)), jnp.tile(sin, (b, 1))
    qkv, k_rows, v_rows = _proj(
        xf, w["norm_mix"][1], w["diff_w_in"], cos, sin,
        outs=[(_BF16, 0, 3 * hd_all), (_F32, hd_all, 2 * hd_all), (_F32, 2 * hd_all, 3 * hd_all)],
        half=_DIFF_HD // 2, n_rope_cols=2 * hd_all, n_first_cols=hd_all,
        scale_first=_DIFF_HD ** -0.5 * math.log2(math.e),
        scale_rest=1.0, tm=tm, tn=1024)
    n_heads = d // (2 * _DIFF_HD)
    if past_k is None:
        o = _diff_attention(qkv, qkv, qkv, 0, n_heads, 2 * n_heads, w["diff_lambda"], w["diff_subln"], b, t, p0,
                            lam_init, n_heads, tq=512, tk=512)
    else:
        n_past = past_k.shape[1]
        n_keys = n_past + t
        k_new = qkv[:, hd_all:2 * hd_all].reshape(b, t, d)
        v_new = qkv[:, 2 * hd_all:].reshape(b, t, d)
        k_all = jnp.concatenate([past_k.reshape(b, n_past, d).astype(_BF16), k_new], axis=1)
        v_all = jnp.concatenate([past_v.reshape(b, n_past, d).astype(_BF16), v_new], axis=1)
        o = _diff_attention(qkv, k_all.reshape(b * n_keys, d), v_all.reshape(b * n_keys, d), 0, 0, 0,
                            w["diff_lambda"], w["diff_subln"], b, t, p0, lam_init, n_heads, tq=t, tk=512)
    xf = _out_proj(o, w["diff_w_out"], xf, tm=tm)
    xf = _cross_attention(xf, w["norm_cross"][1], w["cross_wq"][1], w["cross_wo"][1],
                          mem_k[1].reshape(b, -1, d), mem_v[1].reshape(b, -1, d), b, t, tm=tm)
    y = _moe_final(xf, w["norm_ffn"][1], w["moe_router"], w["moe_w_gu"], w["moe_w_down"], w["norm_final"],
                   bm=512 if big else 128)
    hw = 2 * _DIFF_HD
    return (y.reshape(b, t, d), ret_state, k_rows.reshape(b, t, n_heads, hw), v_rows.reshape(b, t, n_heads, hw))


def kernel(x_prompt, x_sample, state_ret, cache_diff_k, cache_diff_v, cache_mem_k, cache_mem_v, mem_prompt, norm_mix,
           norm_cross, norm_mem, norm_ffn, norm_final, ret_w_in, ret_w_out, diff_w_in, diff_lambda, diff_subln,
           diff_w_out, cross_wq, cross_wkv, cross_wo, ffn_w_gu, ffn_w_down, moe_router, moe_w_gu, moe_w_down):
    depth = norm_mix.shape[0]
    assert depth == 2, "one retention layer followed by one differential-attention layer"
    bf = lambda a: a.astype(_BF16)
    w = dict(norm_mix=norm_mix, norm_cross=norm_cross, norm_ffn=norm_ffn, norm_final=norm_final,
             ret_w_in=bf(ret_w_in[0]), ret_w_out=bf(ret_w_out[0]), diff_w_in=bf(diff_w_in[0]),
             diff_lambda=diff_lambda[0], diff_subln=diff_subln[0], diff_w_out=bf(diff_w_out[0]),
             cross_wq=bf(cross_wq), cross_wo=bf(cross_wo), ffn_w_gu=bf(ffn_w_gu[0]), ffn_w_down=bf(ffn_w_down[0]),
             moe_router=moe_router[0], moe_w_gu=bf(moe_w_gu[0]), moe_w_down=bf(moe_w_down[0]))

    b_p, t_p, d = x_prompt.shape
    b_s, t_s, _ = x_sample.shape
    mem_len = mem_prompt.shape[1]
    mem_flat = mem_prompt.reshape(b_p * mem_len, d)
    mks, mvs = [], []
    for i in range(depth):
        mk, mv = _proj(mem_flat, norm_mem[i], bf(cross_wkv[i]), None, None, outs=[(_F32, 0, d), (_F32, d, 2 * d)],
                       tm=256)
        mks.append(mk.reshape(b_p, mem_len, _CROSS_HEADS, d // _CROSS_HEADS))
        mvs.append(mv.reshape(b_p, mem_len, _CROSS_HEADS, d // _CROSS_HEADS))
    mem_k_p, mem_v_p = jnp.stack(mks), jnp.stack(mvs)

    dkr = d // _RET_HEADS
    s0 = jnp.zeros((b_p, _RET_HEADS, dkr, 2 * dkr), _F32)
    y_p, rs_p, dk_p, dv_p = _trunk(x_prompt, jnp.arange(t_p, dtype=jnp.int32), 0, s0, None, None, mem_k_p, mem_v_p, w,
                                   big=True)
    n_past = cache_diff_k.shape[2]
    y_s, rs_s, dk_s, dv_s = _trunk(x_sample, n_past + jnp.arange(t_s, dtype=jnp.int32), n_past, state_ret[0],
                                   cache_diff_k[0], cache_diff_v[0], cache_mem_k, cache_mem_v, w, big=False)
    return (y_p, y_s, rs_p[None], dk_p[None], dv_p[None], mem_k_p, mem_v_p, rs_s[None], dk_s[None], dv_s[None])
```

```python
import functools
import math

import jax
import jax.numpy as jnp
from jax import lax
from jax.experimental import pallas as pl
from jax.experimental.pallas import tpu as pltpu

_CHUNK = 64
_ROPE_THETA = 10000.0
_EPS = 1e-6
_RET_HEADS = 4
_DIFF_HD = 64
_CROSS_HEADS = 4
_N_EXPERTS = 8
_TOP_K = 2

_VMEM_LIMIT_V7X = 56 * 1024 * 1024
_LANES = 128

_F32 = jnp.float32
_BF16 = jnp.bfloat16
_NT = (((1,), (1,)), ((), ()))
_TN = (((0,), (0,)), ((), ()))
_NEG = -1e30


def _params(*sem):
    return pltpu.CompilerParams(dimension_semantics=sem, vmem_limit_bytes=_VMEM_LIMIT_V7X)


def _tile(n, pref):
    t = min(n, pref)
    assert n % t == 0, (n, pref)
    return t


def _rms(x):
    return x * lax.rsqrt(jnp.mean(x * x, axis=-1, keepdims=True) + _EPS)


def _silu(x):
    return x * jax.nn.sigmoid(x)


def _proj_kernel(x_ref, g_ref, w_ref, cos_ref, sin_ref, rot_ref, *rest, half, n_rope, n_first, scale_first,
                 scale_rest, out_ranges):
    o_refs, xn_ref = rest[:-1], rest[-1]
    j = pl.program_id(1)

    @pl.when(j == 0)
    def _():
        xn_ref[...] = (_rms(x_ref[...]) * g_ref[...]).astype(_BF16)

    acc = jnp.dot(xn_ref[...], w_ref[...], preferred_element_type=_F32)
    tn = acc.shape[1]

    def emit(val, lo, hi):
        for o_ref, (olo, ohi) in zip(o_refs, out_ranges):
            if olo >= hi or ohi <= lo:
                continue
            if olo <= lo and hi <= ohi:
                o_ref[...] = val.astype(o_ref.dtype)
            else:
                @pl.when((j >= olo) & (j < ohi))
                def _(o_ref=o_ref):
                    o_ref[...] = val.astype(o_ref.dtype)

    if n_rope > 0:
        @pl.when(j < n_rope)
        def _():
            cos = cos_ref[...]
            sin = sin_ref[...]
            pieces = []
            if half == _LANES:
                for c in range(tn // (2 * _LANES)):
                    x1 = acc[:, 2 * c * _LANES:(2 * c + 1) * _LANES]
                    x2 = acc[:, (2 * c + 1) * _LANES:(2 * c + 2) * _LANES]
                    pieces += [x1 * cos - x2 * sin, x2 * cos + x1 * sin]
            else:
                wr = rot_ref.shape[0]
                cos_w = jnp.concatenate([cos] * (wr // _LANES), axis=1)
                sin_w = jnp.concatenate([sin] * (wr // _LANES), axis=1)
                for c in range(tn // wr):
                    xc = acc[:, c * wr:(c + 1) * wr]
                    rot = jnp.dot(xc.astype(_BF16), rot_ref[...], preferred_element_type=_F32)
                    pieces.append(xc * cos_w + rot * sin_w)
            r = jnp.concatenate(pieces, axis=1)
            scale = jnp.where(j < n_first, scale_first, scale_rest).astype(_F32)
            emit(r * scale, 0, n_rope)

        @pl.when(j >= n_rope)
        def _():
            emit(acc, n_rope, 1 << 30)
    else:
        emit(acc, 0, 1 << 30)


def _proj(x, gain, w, cos, sin, *, outs, half=_LANES, n_rope_cols=0, n_first_cols=0, scale_first=1.0,
          scale_rest=1.0, tm=512, tn=512):
    n, d = x.shape
    d_out = w.shape[1]
    tm = _tile(n, tm)
    tn = _tile(d_out, tn)
    assert n_rope_cols % tn == 0 and n_first_cols % tn == 0
    if cos is None:
        cos = jnp.zeros((tm, _LANES), _F32)
        sin = cos
    r_tiles = cos.shape[0] // tm
    assert cos.shape[0] % tm == 0
    if n_rope_cols > 0 and half != _LANES:
        wr = 2 * _LANES
        dst = jnp.arange(wr)
        partner = jnp.where(dst % (2 * half) < half, dst + half, dst - half)
        rot = (jnp.arange(wr)[:, None] == partner[None, :]).astype(_BF16)
    else:
        rot = jnp.zeros((8, _LANES), _BF16)
    out_ranges = []
    for _, lo, hi in outs:
        assert lo % tn == 0 and hi % tn == 0
        out_ranges.append((lo // tn, hi // tn))
    kern = functools.partial(_proj_kernel, half=half, n_rope=n_rope_cols // tn, n_first=n_first_cols // tn,
                             scale_first=scale_first, scale_rest=scale_rest, out_ranges=tuple(out_ranges))

    def out_map(lo, hi):
        return lambda i, j: (i, jnp.clip(j - lo, 0, hi - lo - 1))

    return pl.pallas_call(
        kern,
        grid=(n // tm, d_out // tn),
        in_specs=[
            pl.BlockSpec((tm, d), lambda i, j: (i, 0)),
            pl.BlockSpec((1, d), lambda i, j: (0, 0)),
            pl.BlockSpec((d, tn), lambda i, j: (0, j)),
            pl.BlockSpec((tm, _LANES), lambda i, j: (i % r_tiles, 0)),
            pl.BlockSpec((tm, _LANES), lambda i, j: (i % r_tiles, 0)),
            pl.BlockSpec(rot.shape, lambda i, j: (0, 0)),
        ],
        out_specs=[pl.BlockSpec((tm, tn), out_map(lo, hi)) for lo, hi in out_ranges],
        out_shape=[jax.ShapeDtypeStruct((n, hi - lo), dt) for dt, lo, hi in outs],
        scratch_shapes=[pltpu.VMEM((tm, d), _BF16)],
        compiler_params=_params("parallel", "arbitrary"),
        name="norm_proj",
    )(x, gain.reshape(1, d), w, cos, sin, rot)


def _retention_kernel(q_ref, k_ref, v_ref, g_ref, s0_ref, dm_ref, qd_ref, kd_ref, cd_ref, x_ref, wo_ref, xo_ref, so_ref,
                      s_ref, og_ref):
    n = pl.program_id(1)
    heads, dk, dv = s_ref.shape

    @pl.when(n == 0)
    def _():
        s_ref[...] = s0_ref[0]

    for hh in range(heads):
        q = q_ref[:, hh * dk:(hh + 1) * dk]
        k = k_ref[:, hh * dk:(hh + 1) * dk]
        v = v_ref[:, hh * dv:(hh + 1) * dv]
        sc = lax.dot_general(q, k, _NT, preferred_element_type=_F32)
        p = (sc * dm_ref[hh]).astype(_BF16)
        s_old = s_ref[hh]
        q_dec = (q.astype(_F32) * qd_ref[hh]).astype(_BF16)
        o = jnp.dot(p, v, preferred_element_type=_F32) + jnp.dot(q_dec, s_old.astype(_BF16),
                                                                preferred_element_type=_F32)
        k_dec = (k.astype(_F32) * kd_ref[hh]).astype(_BF16)
        s_ref[hh] = s_old * cd_ref[hh] + lax.dot_general(k_dec, v, _TN, preferred_element_type=_F32)
        g = g_ref[:, hh * dv:(hh + 1) * dv].astype(_F32)
        og_ref[:, hh * dv:(hh + 1) * dv] = (_silu(g) * _rms(o)).astype(og_ref.dtype)

    xo_ref[...] = x_ref[...] + jnp.dot(og_ref[...], wo_ref[...], preferred_element_type=_F32)

    @pl.when(n == pl.num_programs(1) - 1)
    def _():
        so_ref[0] = s_ref[...]


def _retention(qkvg, s0, x, w_out, batch, t, *, block=256):
    d = x.shape[1]
    h = _RET_HEADS
    dk, dv = s0.shape[2], s0.shape[3]
    lb = _tile(t, block)
    nb = t // lb
    assert lb % _CHUNK == 0
    log_g = jnp.log1p(-(2.0 ** (-5.0 - jnp.arange(h, dtype=_F32))))
    i = jnp.arange(lb, dtype=_F32)
    ci = jnp.arange(lb) // _CHUNK
    diff = i[:, None] - i[None, :]
    same = ci[:, None] == ci[None, :]
    before = ci[None, :] < ci[:, None]
    expo = jnp.where(same, jnp.abs(diff), diff)
    dmask = jnp.where((same | before)[None], jnp.exp(expo[None] * log_g[:, None, None]), 0.0).astype(_F32)
    q_dec = jnp.exp((i[None, :] + 1.0) * log_g[:, None])[..., None]
    k_dec = jnp.exp((lb - 1.0 - i)[None, :] * log_g[:, None])[..., None]
    c_dec = jnp.exp(lb * log_g).reshape(h, 1, 1)
    wq, wv = h * dk, h * dv
    assert (2 * wq) % wv == 0
    v_blk = (2 * wq) // wv
    return pl.pallas_call(
        _retention_kernel,
        grid=(batch, nb),
        in_specs=[
            pl.BlockSpec((lb, wq), lambda b, n: (b * nb + n, 0)),
            pl.BlockSpec((lb, wq), lambda b, n: (b * nb + n, 1)),
            pl.BlockSpec((lb, wv), lambda b, n: (b * nb + n, v_blk)),
            pl.BlockSpec((lb, wv), lambda b, n: (b * nb + n, v_blk + 1)),
            pl.BlockSpec((1, h, dk, dv), lambda b, n: (b, 0, 0, 0)),
            pl.BlockSpec((h, lb, lb), lambda b, n: (0, 0, 0)),
            pl.BlockSpec((h, lb, 1), lambda b, n: (0, 0, 0)),
            pl.BlockSpec((h, lb, 1), lambda b, n: (0, 0, 0)),
            pl.BlockSpec((h, 1, 1), lambda b, n: (0, 0, 0)),
            pl.BlockSpec((lb, d), lambda b, n: (b * nb + n, 0)),
            pl.BlockSpec((wv, d), lambda b, n: (0, 0)),
        ],
        out_specs=[
            pl.BlockSpec((lb, d), lambda b, n: (b * nb + n, 0)),
            pl.BlockSpec((1, h, dk, dv), lambda b, n: (b, 0, 0, 0)),
        ],
        out_shape=[
            jax.ShapeDtypeStruct((batch * t, d), _F32),
            jax.ShapeDtypeStruct((batch, h, dk, dv), _F32),
        ],
        scratch_shapes=[pltpu.VMEM((h, dk, dv), _F32), pltpu.VMEM((lb, wv), _BF16)],
        compiler_params=_params("parallel", "arbitrary"),
        name="retention",
    )(qkvg, qkvg, qkvg, qkvg, s0, dmask, q_dec, k_dec, c_dec, x, w_out)


def _route_top2(xn, w_router_t):
    logits = lax.dot_general(w_router_t, xn, _NT, preferred_element_type=_F32, precision=lax.Precision.HIGHEST)
    ne = logits.shape[0]
    row = lax.broadcasted_iota(jnp.int32, logits.shape, 0)
    v1 = jnp.max(logits, axis=0, keepdims=True)
    i1 = jnp.min(jnp.where(logits == v1, row, ne), axis=0, keepdims=True)
    rest = jnp.where(row == i1, -jnp.inf, logits)
    v2 = jnp.max(rest, axis=0, keepdims=True)
    i2 = jnp.min(jnp.where(rest == v2, row, ne), axis=0, keepdims=True)
    e2 = jnp.exp(v2 - v1)
    w1 = 1.0 / (1.0 + e2)
    return jnp.concatenate([i1, i2], axis=0), jnp.concatenate([w1, e2 * w1], axis=0)


def _cross_kernel(*refs, heads, pre, route):
    it = iter(refs)
    x_ref = next(it)
    a_ref, wp_ref = (next(it), next(it)) if pre else (None, None)
    g_ref, wq_ref, wo_ref, mk_ref, mv_ref = (next(it) for _ in range(5))
    gr_ref, wr_ref = (next(it), next(it)) if route else (None, None)
    o_ref = next(it)
    x = x_ref[...]
    if pre:
        x = x + jnp.dot(a_ref[...], wp_ref[...], preferred_element_type=_F32)
    d = x.shape[1]
    hd = d // heads
    xn = (_rms(x) * g_ref[...]).astype(_BF16)
    q = (jnp.dot(xn, wq_ref[...], preferred_element_type=_F32) * (hd ** -0.5)).astype(_BF16)
    mk = mk_ref[0].astype(_BF16)
    mv = mv_ref[0].astype(_BF16)
    outs = []
    for hh in range(heads):
        sl = slice(hh * hd, (hh + 1) * hd)
        s = lax.dot_general(q[:, sl], mk[:, sl], _NT, preferred_element_type=_F32)
        s = s - jnp.max(s, axis=-1, keepdims=True)
        e = jnp.exp(s)
        p = (e / jnp.sum(e, axis=-1, keepdims=True)).astype(_BF16)
        outs.append(jnp.dot(p, mv[:, sl], preferred_element_type=_F32).astype(_BF16))
    o = jnp.concatenate(outs, axis=1)
    x_new = x + jnp.dot(o, wo_ref[...], preferred_element_type=_F32)
    o_ref[...] = x_new
    if route:
        idx_ref, wgt_ref = next(it), next(it)
        idx, wgt = _route_top2(_rms(x_new) * gr_ref[...], wr_ref[...])
        idx_ref[...] = idx
        wgt_ref[...] = wgt


def _cross_attention(x, gain, wq, wo, mk, mv, batch, t, *, tm=512, pre=None, route=None):
    n, d = x.shape
    tm = _tile(t, tm)
    nt = t // tm
    mem = mk.shape[1]
    row = lambda b, i: (b * nt + i, 0)
    const = lambda b, i: (0, 0)
    args, specs = [x], [pl.BlockSpec((tm, d), row)]
    if pre is not None:
        a, w_pre = pre
        args += [a, w_pre]
        specs += [pl.BlockSpec((tm, a.shape[1]), row), pl.BlockSpec(w_pre.shape, const)]
    args += [gain.reshape(1, d), wq, wo, mk, mv]
    specs += [pl.BlockSpec((1, d), const), pl.BlockSpec((d, d), const), pl.BlockSpec((d, d), const),
              pl.BlockSpec((1, mem, d), lambda b, i: (b, 0, 0)), pl.BlockSpec((1, mem, d), lambda b, i: (b, 0, 0))]
    out_specs = [pl.BlockSpec((tm, d), row)]
    out_shape = [jax.ShapeDtypeStruct((n, d), _F32)]
    if route is not None:
        gain_r, w_router_t = route
        args += [gain_r.reshape(1, d), w_router_t]
        specs += [pl.BlockSpec((1, d), const), pl.BlockSpec(w_router_t.shape, const)]
        out_specs += [pl.BlockSpec((_TOP_K, tm), lambda b, i: (0, b * nt + i))] * 2
        out_shape += [jax.ShapeDtypeStruct((_TOP_K, n), jnp.int32), jax.ShapeDtypeStruct((_TOP_K, n), _F32)]
    return pl.pallas_call(
        functools.partial(_cross_kernel, heads=_CROSS_HEADS, pre=pre is not None, route=route is not None),
        grid=(batch, nt),
        in_specs=specs,
        out_specs=out_specs,
        out_shape=out_shape,
        compiler_params=_params("parallel", "parallel"),
        name="cross_attention",
    )(*args)


def _ffn_kernel(x_ref, g_ref, wg_ref, wu_ref, wd_ref, o_ref, xn_ref, acc_ref):
    j = pl.program_id(1)

    @pl.when(j == 0)
    def _():
        xn_ref[...] = (_rms(x_ref[...]) * g_ref[...]).astype(_BF16)
        acc_ref[...] = x_ref[...]

    xn = xn_ref[...]
    gate = jnp.dot(xn, wg_ref[...], preferred_element_type=_F32)
    up = jnp.dot(xn, wu_ref[...], preferred_element_type=_F32)
    act = (_silu(gate) * up).astype(_BF16)
    acc_ref[...] += jnp.dot(act, wd_ref[...], preferred_element_type=_F32)

    @pl.when(j == pl.num_programs(1) - 1)
    def _():
        o_ref[...] = acc_ref[...]


def _dense_ffn(x, gain, w_gu, w_down, *, tm=512, tf=1408):
    n, d = x.shape
    f = w_down.shape[0]
    tm = _tile(n, tm)
    tf = _tile(f, tf)
    nf = f // tf
    return pl.pallas_call(
        _ffn_kernel,
        grid=(n // tm, nf),
        in_specs=[
            pl.BlockSpec((tm, d), lambda i, j: (i, 0)),
            pl.BlockSpec((1, d), lambda i, j: (0, 0)),
            pl.BlockSpec((d, tf), lambda i, j: (0, j)),
            pl.BlockSpec((d, tf), lambda i, j: (0, nf + j)),
            pl.BlockSpec((tf, d), lambda i, j: (j, 0)),
        ],
        out_specs=pl.BlockSpec((tm, d), lambda i, j: (i, 0)),
        out_shape=jax.ShapeDtypeStruct((n, d), _F32),
        scratch_shapes=[pltpu.VMEM((tm, d), _BF16), pltpu.VMEM((tm, d), _F32)],
        compiler_params=_params("parallel", "arbitrary"),
        name="dense_swiglu",
    )(x, gain.reshape(1, d), w_gu, w_gu, w_down)


def _diff_attn_kernel(q_ref, k_ref, v_ref, *rest, p0, tq, tk, lam_init, own_keys):
    if own_keys:
        kd_ref, vd_ref = rest[:2]
        rest = rest[2:]
    else:
        kd_ref, vd_ref = k_ref, v_ref
    lam_ref, sub_ref, o_ref, q2_ref, m_ref, acc_ref, s_ref = rest
    qi = pl.program_id(2)
    hd = _DIFF_HD
    hw = 2 * hd

    q = q_ref[...]
    first = lax.broadcasted_iota(jnp.int32, (1, hw), 1) < hd
    q2_ref[0:tq, :] = jnp.where(first, q, 0.0).astype(_BF16)
    q2_ref[tq:2 * tq, :] = jnp.where(first, 0.0, q).astype(_BF16)
    m_ref[...] = jnp.full(m_ref.shape, _NEG, _F32)
    acc_ref[...] = jnp.zeros(acc_ref.shape, _F32)

    def scores(off, nk, ref=k_ref):
        k_blk = ref[pl.ds(off, nk), :].astype(_BF16)
        return lax.dot_general(q2_ref[...], k_blk, _NT, preferred_element_type=_F32)

    def softmax_pv(s, off, nk, visible, ref=v_ref):
        v_blk = ref[pl.ds(off, nk), :].astype(_BF16)
        v_ext = jnp.concatenate([v_blk, jnp.ones(v_blk.shape, v_blk.dtype)], axis=1)
        if visible is not None:
            s = jnp.where(visible, s, _NEG)
        m_old = m_ref[...]
        m_new = jnp.maximum(m_old, jnp.max(s, axis=-1, keepdims=True))
        alpha = jnp.exp2(m_old - m_new)
        if nk % _LANES == 0:
            p = jnp.exp2(s - jnp.concatenate([m_new] * (nk // _LANES), axis=1))
        else:
            p = jnp.exp2(s - m_new[:, 0:1])
        pv = jnp.dot(p.astype(_BF16), v_ext, preferred_element_type=_F32)
        acc_ref[...] = jnp.concatenate([alpha, alpha], axis=1) * acc_ref[...] + pv
        m_ref[...] = m_new

    q_start = p0 + qi * tq
    n_full = q_start // tk
    r_chunk = lax.broadcasted_iota(jnp.int32, (2 * tq, tq), 0) % tq // _CHUNK
    c_chunk = lax.broadcasted_iota(jnp.int32, (2 * tq, tq), 1) // _CHUNK
    vis = c_chunk <= r_chunk
    diag_off = pl.multiple_of(q_start, tq)

    if tq == tk and not own_keys:
        s_ref[0] = scores(0, tk)

        def pair(j):
            off0 = pl.multiple_of(2 * j * tk, tk)
            off1 = pl.multiple_of(off0 + tk, tk)
            s_ref[1] = scores(off1, tk)
            softmax_pv(s_ref[0], off0, tk, None)
            s_ref[0] = scores(pl.multiple_of(off1 + tk, tk), tk)
            softmax_pv(s_ref[1], off1, tk, None)

        n_pairs = n_full // 2
        group = _ATTN_PAIRS_PER_TRIP

        def body(g, carry):
            for u in range(group):
                pair(g * group + u)
            return carry

        lax.fori_loop(0, n_pairs // group, body, 0)
        done = n_pairs // group * group
        width = group // 2
        while width >= 1:
            @pl.when((n_pairs - done) % (2 * width) >= width)
            def _(done=done, width=width):
                for u in range(width):
                    pair(done + u)
            done = done + jnp.where((n_pairs - done) % (2 * width) >= width, width, 0)
            width //= 2

        @pl.when(n_full % 2 == 1)
        def _():
            s_ref[1] = scores(diag_off, tk)
            softmax_pv(s_ref[0], pl.multiple_of(diag_off - tk, tk), tk, None)
            softmax_pv(s_ref[1], diag_off, tq, vis)

        @pl.when(n_full % 2 == 0)
        def _():
            softmax_pv(s_ref[0], diag_off, tq, vis)
    else:
        def body(i, carry):
            off = pl.multiple_of(i * tk, tk)
            softmax_pv(scores(off, tk), off, tk, None)
            return carry

        lax.fori_loop(0, n_full, body, 0)
        own_off = pl.multiple_of(qi * tq, tq) if own_keys else diag_off
        softmax_pv(scores(own_off, tq, kd_ref), own_off, tq, vis, vd_ref)

    lp = lam_ref[...]
    lam = (jnp.exp(jnp.sum(lp[0:1] * lp[1:2], axis=-1, keepdims=True))
           - jnp.exp(jnp.sum(lp[2:3] * lp[3:4], axis=-1, keepdims=True)) + lam_init)
    acc = acc_ref[...]
    o = acc[0:tq, 0:hw] / acc[0:tq, hw:2 * hw] - lam * (acc[tq:2 * tq, 0:hw] / acc[tq:2 * tq, hw:2 * hw])
    o_ref[...] = (_rms(o) * sub_ref[...] * (1.0 - lam_init)).astype(o_ref.dtype)


def _diff_attention(q_arr, k_arr, v_arr, q_col, k_col, v_col, lam_p, subln, batch, t, p0, lam_init, n_heads, *, tq,
                    tk, past=None):
    hw = 2 * _DIFF_HD
    tq = _tile(t, tq)
    nq = t // tq
    tk = math.gcd(tk, p0, tq) if nq > 1 else math.gcd(tk, p0)
    assert tq % _CHUNK == 0 and tk % _CHUNK == 0 and (p0 > 0) == (past is not None)
    own_keys = past is not None
    kern = functools.partial(_diff_attn_kernel, p0=p0, tq=tq, tk=tk, lam_init=lam_init, own_keys=own_keys)
    own_specs = [pl.BlockSpec((t, hw), lambda b, hh, qi: (b, k_col + hh)),
                 pl.BlockSpec((t, hw), lambda b, hh, qi: (b, v_col + hh))]
    if own_keys:
        key_args = [past[0], past[1], k_arr, v_arr]
        key_specs = [pl.BlockSpec((p0, hw), lambda b, hh, qi: (b, hh)),
                     pl.BlockSpec((p0, hw), lambda b, hh, qi: (b, hh))] + own_specs
    else:
        key_args, key_specs = [k_arr, v_arr], own_specs
    return pl.pallas_call(
        kern,
        grid=(batch, n_heads, nq),
        in_specs=[pl.BlockSpec((tq, hw), lambda b, hh, qi: (b * nq + qi, q_col + hh))] + key_specs + [
            pl.BlockSpec(lam_p.shape, lambda b, hh, qi: (0, 0)),
            pl.BlockSpec((1, hw), lambda b, hh, qi: (0, 0)),
        ],
        out_specs=pl.BlockSpec((tq, hw), lambda b, hh, qi: (b * nq + qi, hh)),
        out_shape=jax.ShapeDtypeStruct((batch * t, n_heads * hw), _BF16),
        scratch_shapes=[
            pltpu.VMEM((2 * tq, hw), _BF16),
            pltpu.VMEM((2 * tq, _LANES), _F32),
            pltpu.VMEM((2 * tq, 2 * hw), _F32),
            pltpu.VMEM((2, 2 * tq, tk) if tq == tk and not own_keys else (2, 8, _LANES), _F32),
        ],
        compiler_params=_params("parallel", "parallel", "arbitrary"),
        name="diff_attention",
    )(q_arr, *key_args, lam_p, subln.reshape(1, hw))


def _router_kernel(x_ref, g_ref, wr_ref, idx_ref, wgt_ref):
    idx, wgt = _route_top2(_rms(x_ref[...]) * g_ref[...], wr_ref[...])
    idx_ref[...] = idx
    wgt_ref[...] = wgt


def _router(x, gain, w_router_t, *, tm=512):
    n, d = x.shape
    ne = w_router_t.shape[0]
    tm = _tile(n, tm)
    return pl.pallas_call(
        _router_kernel,
        grid=(n // tm,),
        in_specs=[
            pl.BlockSpec((tm, d), lambda i: (i, 0)),
            pl.BlockSpec((1, d), lambda i: (0, 0)),
            pl.BlockSpec((ne, d), lambda i: (0, 0)),
        ],
        out_specs=[
            pl.BlockSpec((_TOP_K, tm), lambda i: (0, i)),
            pl.BlockSpec((_TOP_K, tm), lambda i: (0, i)),
        ],
        out_shape=[
            jax.ShapeDtypeStruct((_TOP_K, n), jnp.int32),
            jax.ShapeDtypeStruct((_TOP_K, n), _F32),
        ],
        compiler_params=_params("parallel"),
        name="moe_router",
    )(x, gain.reshape(1, d), w_router_t)


_GATHER_UNROLL = 8
_ATTN_PAIRS_PER_TRIP = 4


def _start_row_gather(src_hbm, dst_vmem, idx_at, sem, *, static_rows=False):
    def start(r, c):
        pltpu.make_async_copy(src_hbm.at[pl.ds(idx_at(r), 1), :], dst_vmem.at[pl.ds(r, 1), :], sem).start()
        return c

    if static_rows:
        for r in range(dst_vmem.shape[0]):
            start(r, 0)
    else:
        lax.fori_loop(0, dst_vmem.shape[0], start, 0, unroll=_GATHER_UNROLL)


def _wait_row_gather(src_hbm, dst_vmem, sem):
    pltpu.make_async_copy(src_hbm.at[pl.ds(0, dst_vmem.shape[0]), :], dst_vmem, sem).wait()


def _expert_kernel(be_ref, nu_ref, rt_ref, rt_next_ref, x_hbm, g_ref, wg_ref, wu_ref, wd_ref, o_ref, xg_ref, xn_ref,
                   acc_ref, sem, *, rows_per_step):
    del be_ref
    i = pl.program_id(0)
    j = pl.program_id(1)
    n_used = nu_ref[0]
    active = i < n_used
    slot = lax.rem(i, 2)

    @pl.when(active & (i == 0) & (j == 0))
    def _():
        _start_row_gather(x_hbm, xg_ref.at[0], lambda r: rt_ref[0, 0, r], sem.at[0])

    @pl.when((i <= n_used) & (j == 0))
    def _():
        _wait_row_gather(x_hbm, xg_ref.at[slot], sem.at[slot])

    @pl.when(active & (j == 0))
    def _():
        xn_ref[...] = (_rms(xg_ref[slot]) * g_ref[...]).astype(_BF16)
        acc_ref[...] = jnp.zeros(acc_ref.shape, _F32)

    @pl.when(active)
    def _():
        base = j * rows_per_step
        for c in range(rows_per_step):
            r = base + c
            pltpu.make_async_copy(x_hbm.at[pl.ds(rt_next_ref[0, 0, r], 1), :],
                                  xg_ref.at[1 - slot, pl.ds(r, 1), :], sem.at[1 - slot]).start()
        xn = xn_ref[...]
        gate = jnp.dot(xn, wg_ref[0], preferred_element_type=_F32)
        up = jnp.dot(xn, wu_ref[0], preferred_element_type=_F32)
        act = (_silu(gate) * up).astype(_BF16)
        acc_ref[...] += jnp.dot(act, wd_ref[0], preferred_element_type=_F32)

    last = j == pl.num_programs(1) - 1

    @pl.when(active & last)
    def _():
        o_ref[...] = acc_ref[...]

    @pl.when(jnp.logical_not(active) & last)
    def _():
        o_ref[...] = jnp.zeros(o_ref.shape, _F32)


def _experts(x, gain, w_gu, w_down, row_tok, block_e, n_used, *, bm, tf=1792):
    n, d = x.shape
    f = w_down.shape[1]
    tf = _tile(f, tf)
    nf = f // tf
    n_blocks = block_e.shape[0]
    assert bm % nf == 0

    def col(i, j, nu):
        return jnp.where(i < nu[0], j, nf - 1)

    grid_spec = pltpu.PrefetchScalarGridSpec(
        num_scalar_prefetch=2,
        grid=(n_blocks, nf),
        in_specs=[
            pl.BlockSpec((1, 1, bm), lambda i, j, be, nu: (i, 0, 0), memory_space=pltpu.SMEM),
            pl.BlockSpec((1, 1, bm), lambda i, j, be, nu: (jnp.minimum(i + 1, n_blocks - 1), 0, 0),
                         memory_space=pltpu.SMEM),
            pl.BlockSpec(memory_space=pl.ANY),
            pl.BlockSpec((1, d), lambda i, j, be, nu: (0, 0)),
            pl.BlockSpec((1, d, tf), lambda i, j, be, nu: (be[i], 0, col(i, j, nu))),
            pl.BlockSpec((1, d, tf), lambda i, j, be, nu: (be[i], 0, nf + col(i, j, nu))),
            pl.BlockSpec((1, tf, d), lambda i, j, be, nu: (be[i], col(i, j, nu), 0)),
        ],
        out_specs=pl.BlockSpec((bm, d), lambda i, j, be, nu: (i, 0)),
        scratch_shapes=[
            pltpu.VMEM((2, bm, d), _F32),
            pltpu.VMEM((bm, d), _BF16),
            pltpu.VMEM((bm, d), _F32),
            pltpu.SemaphoreType.DMA((2,)),
        ],
    )
    row_tok_b = row_tok.reshape(n_blocks, 1, bm)
    return pl.pallas_call(
        functools.partial(_expert_kernel, rows_per_step=bm // nf),
        grid_spec=grid_spec,
        out_shape=jax.ShapeDtypeStruct((n_blocks * bm, d), _F32),
        compiler_params=_params("arbitrary", "arbitrary"),
        name="moe_experts",
    )(block_e, n_used, row_tok_b, row_tok_b, x, gain.reshape(1, d), w_gu, w_gu, w_down)


def _combine_kernel(d_ref, d_next_ref, x_ref, w_ref, y_hbm, gf_ref, o_ref, r_ref, sem):
    i = pl.program_id(0)
    slot = lax.rem(i, 2)

    def start(dref, s):
        for kk in range(_TOP_K):
            _start_row_gather(y_hbm, r_ref.at[s, kk], lambda r, kk=kk: dref[0, kk, r], sem.at[s],
                              static_rows=isinstance(s, int))

    @pl.when(i == 0)
    def _():
        start(d_ref, 0)

    for s_next in range(2):
        @pl.when((i + 1 < pl.num_programs(0)) & (slot == 1 - s_next))
        def _(s_next=s_next):
            start(d_next_ref, s_next)

    for kk in range(_TOP_K):
        _wait_row_gather(y_hbm, r_ref.at[slot, kk], sem.at[slot])
    w = w_ref[...]
    x = x_ref[...]
    for kk in range(_TOP_K):
        x = x + w[:, kk:kk + 1] * r_ref[slot, kk]
    o_ref[...] = _rms(x) * gf_ref[...]


def _combine_final(x, dest, wgt, yb, gain_final, *, tm=256):
    n, d = x.shape
    tm = _tile(n, tm)
    nt = n // tm
    dest_b = dest.reshape(nt, tm, _TOP_K).transpose(0, 2, 1)
    return pl.pallas_call(
        _combine_kernel,
        grid=(nt,),
        in_specs=[
            pl.BlockSpec((1, _TOP_K, tm), lambda i: (i, 0, 0), memory_space=pltpu.SMEM),
            pl.BlockSpec((1, _TOP_K, tm), lambda i: (jnp.minimum(i + 1, nt - 1), 0, 0), memory_space=pltpu.SMEM),
            pl.BlockSpec((tm, d), lambda i: (i, 0)),
            pl.BlockSpec((tm, _TOP_K), lambda i: (i, 0)),
            pl.BlockSpec(memory_space=pl.ANY),
            pl.BlockSpec((1, d), lambda i: (0, 0)),
        ],
        out_specs=pl.BlockSpec((tm, d), lambda i: (i, 0)),
        out_shape=jax.ShapeDtypeStruct((n, d), _F32),
        scratch_shapes=[
            pltpu.VMEM((2, _TOP_K, tm, d), _F32),
            pltpu.SemaphoreType.DMA((2,)),
        ],
        compiler_params=_params("arbitrary"),
        name="moe_combine_final_norm",
    )(dest_b, dest_b, x, wgt, yb, gain_final.reshape(1, d))


def _moe_final(x, idx, wgt, gain, w_gu, w_down, gain_final, *, bm):
    n, d = x.shape
    ne = w_gu.shape[0]
    a = n * _TOP_K
    e_flat = idx.T.reshape(a)
    onehot = (e_flat[:, None] == jnp.arange(ne, dtype=jnp.int32)[None, :]).astype(jnp.int32)
    csum = jnp.cumsum(onehot, axis=0)
    rank = jnp.sum(csum * onehot, axis=1) - 1
    counts = csum[-1]
    padded = (counts + bm - 1) // bm * bm
    pad_end = jnp.cumsum(padded)
    pad_start = pad_end - padded
    dest = pad_start[e_flat] + rank
    assert a % bm == 0
    n_blocks = a // bm + ne
    row_tok = jnp.zeros((n_blocks * bm,), jnp.int32).at[dest].set(jnp.arange(a, dtype=jnp.int32) // _TOP_K)
    block_start = jnp.arange(n_blocks, dtype=jnp.int32) * bm
    block_e = jnp.minimum(jnp.sum((block_start[:, None] >= pad_end[None, :]).astype(jnp.int32), axis=1), ne - 1)
    n_used = (pad_end[-1:] // bm).astype(jnp.int32)
    yb = _experts(x, gain, w_gu, w_down, row_tok, block_e, n_used, bm=bm)
    return _combine_final(x, dest.reshape(n, _TOP_K), wgt.T, yb, gain_final)


def _rope_tables(pos, dim, lanes_signed):
    half = dim // 2
    inv = _ROPE_THETA ** (-jnp.arange(half, dtype=_F32) / half)
    ang = pos.astype(_F32)[:, None] * inv[None, :]
    cos, sin = jnp.cos(ang), jnp.sin(ang)
    if not lanes_signed:
        return cos, sin
    reps = _LANES // dim
    cos_l = jnp.tile(jnp.concatenate([cos, cos], axis=1), (1, reps))
    sin_l = jnp.tile(jnp.concatenate([-sin, sin], axis=1), (1, reps))
    return cos_l, sin_l


def _trunk(x, pos, p0, ret_s0, past_k, past_v, mem_k, mem_v, w, *, big):
    b, t, d = x.shape
    n = b * t
    xf = x.reshape(n, d)
    tm = 512 if big else 256
    dk = d // _RET_HEADS
    hd_all = d

    cos, sin = _rope_tables(pos, dk, False)
    if cos.shape[0] % tm != 0:
        cos, sin = jnp.tile(cos, (b, 1)), jnp.tile(sin, (b, 1))
    (qkvg,) = _proj(xf, w["norm_mix"][0], w["ret_w_in"], cos, sin, outs=[(_BF16, 0, w["ret_w_in"].shape[1])],
                    half=dk // 2, n_rope_cols=2 * d, n_first_cols=d, scale_first=1.0, scale_rest=dk ** -0.5,
                    tm=2 * tm, tn=1024)
    xf, ret_state = _retention(qkvg, ret_s0, xf, w["ret_w_out"], b, t)
    (xf,) = _cross_attention(xf, w["norm_cross"][0], w["cross_wq"][0], w["cross_wo"][0],
                             mem_k[0].reshape(b, -1, d), mem_v[0].reshape(b, -1, d), b, t, tm=tm)
    xf = _dense_ffn(xf, w["norm_ffn"][0], w["ffn_w_gu"], w["ffn_w_down"], tm=tm)

    lam_init = 0.8 - 0.6 * math.exp(-0.3 * 1)
    cos, sin = _rope_tables(pos, _DIFF_HD, True)
    if cos.shape[0] % tm != 0:
        cos, sin = jnp.tile(cos, (b, 1)), jnp.tile(sin, (b, 1))
    qkv, k_rows, v_rows = _proj(
        xf, w["norm_mix"][1], w["diff_w_in"], cos, sin,
        outs=[(_BF16, 0, 3 * hd_all), (_F32, hd_all, 2 * hd_all), (_F32, 2 * hd_all, 3 * hd_all)],
        half=_DIFF_HD // 2, n_rope_cols=2 * hd_all, n_first_cols=hd_all,
        scale_first=_DIFF_HD ** -0.5 * math.log2(math.e),
        scale_rest=1.0, tm=tm, tn=1024)
    n_heads = d // (2 * _DIFF_HD)
    past = None if past_k is None else (past_k.reshape(b * p0, d), past_v.reshape(b * p0, d))
    o = _diff_attention(qkv, qkv, qkv, 0, n_heads, 2 * n_heads, w["diff_lambda"], w["diff_subln"], b, t, p0,
                        lam_init, n_heads, tq=512, tk=512, past=past)
    w_router_t = w["moe_router"].T
    fuse_route = min(t, tm) % _LANES == 0
    res = _cross_attention(xf, w["norm_cross"][1], w["cross_wq"][1], w["cross_wo"][1],
                           mem_k[1].reshape(b, -1, d), mem_v[1].reshape(b, -1, d), b, t, tm=tm,
                           pre=(o, w["diff_w_out"]),
                           route=(w["norm_ffn"][1], w_router_t) if fuse_route else None)
    if fuse_route:
        xf, idx, wgt = res
    else:
        (xf,) = res
        idx, wgt = _router(xf, w["norm_ffn"][1], w_router_t)
    y = _moe_final(xf, idx, wgt, w["norm_ffn"][1], w["moe_w_gu"], w["moe_w_down"], w["norm_final"],
                   bm=512 if big else 128)
    hw = 2 * _DIFF_HD
    return (y.reshape(b, t, d), ret_state, k_rows.reshape(b, t, n_heads, hw), v_rows.reshape(b, t, n_heads, hw))


def kernel(x_prompt, x_sample, state_ret, cache_diff_k, cache_diff_v, cache_mem_k, cache_mem_v, mem_prompt, norm_mix,
           norm_cross, norm_mem, norm_ffn, norm_final, ret_w_in, ret_w_out, diff_w_in, diff_lambda, diff_subln,
           diff_w_out, cross_wq, cross_wkv, cross_wo, ffn_w_gu, ffn_w_down, moe_router, moe_w_gu, moe_w_down):
    depth = norm_mix.shape[0]
    assert depth == 2, "one retention layer followed by one differential-attention layer"
    bf = lambda a: a.astype(_BF16)
    w = dict(norm_mix=norm_mix, norm_cross=norm_cross, norm_ffn=norm_ffn, norm_final=norm_final,
             ret_w_in=bf(ret_w_in[0]), ret_w_out=bf(ret_w_out[0]), diff_w_in=bf(diff_w_in[0]),
             diff_lambda=diff_lambda[0], diff_subln=diff_subln[0], diff_w_out=bf(diff_w_out[0]),
             cross_wq=bf(cross_wq), cross_wo=bf(cross_wo), ffn_w_gu=bf(ffn_w_gu[0]), ffn_w_down=bf(ffn_w_down[0]),
             moe_router=moe_router[0], moe_w_gu=bf(moe_w_gu[0]), moe_w_down=bf(moe_w_down[0]))

    b_p, t_p, d = x_prompt.shape
    b_s, t_s, _ = x_sample.shape
    mem_len = mem_prompt.shape[1]
    mem_flat = mem_prompt.reshape(b_p * mem_len, d)
    mks, mvs = [], []
    for i in range(depth):
        mk, mv = _proj(mem_flat, norm_mem[i], bf(cross_wkv[i]), None, None, outs=[(_F32, 0, d), (_F32, d, 2 * d)],
                       tm=256)
        mks.append(mk.reshape(b_p, mem_len, _CROSS_HEADS, d // _CROSS_HEADS))
        mvs.append(mv.reshape(b_p, mem_len, _CROSS_HEADS, d // _CROSS_HEADS))
    mem_k_p, mem_v_p = jnp.stack(mks), jnp.stack(mvs)

    dkr = d // _RET_HEADS
    s0 = jnp.zeros((b_p, _RET_HEADS, dkr, 2 * dkr), _F32)
    y_p, rs_p, dk_p, dv_p = _trunk(x_prompt, jnp.arange(t_p, dtype=jnp.int32), 0, s0, None, None, mem_k_p, mem_v_p, w,
                                   big=True)
    n_past = cache_diff_k.shape[2]
    y_s, rs_s, dk_s, dv_s = _trunk(x_sample, n_past + jnp.arange(t_s, dtype=jnp.int32), n_past, state_ret[0],
                                   cache_diff_k[0], cache_diff_v[0], cache_mem_k, cache_mem_v, w, big=False)
    return (y_p, y_s, rs_p[None], dk_p[None], dv_p[None], mem_k_p, mem_v_p, rs_s[None], dk_s[None], dv_s[None])
```

```python
import functools
import math

import jax
import jax.numpy as jnp
from jax import lax
from jax.experimental import pallas as pl
from jax.experimental.pallas import tpu as pltpu

_CHUNK = 64
_ROPE_THETA = 10000.0
_EPS = 1e-6
_RET_HEADS = 4
_DIFF_HD = 64
_CROSS_HEADS = 4
_N_EXPERTS = 8
_TOP_K = 2

_VMEM_LIMIT_V7X = 56 * 1024 * 1024
_LANES = 128

_F32 = jnp.float32
_BF16 = jnp.bfloat16
_NT = (((1,), (1,)), ((), ()))
_TN = (((0,), (0,)), ((), ()))
_NEG = -1e30


def _params(*sem):
    return pltpu.CompilerParams(dimension_semantics=sem, vmem_limit_bytes=_VMEM_LIMIT_V7X)


def _tile(n, pref):
    t = min(n, pref)
    assert n % t == 0, (n, pref)
    return t


def _rms(x):
    return x * lax.rsqrt(jnp.mean(x * x, axis=-1, keepdims=True) + _EPS)


def _silu(x):
    return x * jax.nn.sigmoid(x)


def _proj_kernel(x_ref, g_ref, w_ref, cos_ref, sin_ref, rot_ref, *rest, half, n_rope, n_first, scale_first,
                 scale_rest, out_ranges):
    o_refs, xn_ref = rest[:-1], rest[-1]
    j = pl.program_id(1)

    @pl.when(j == 0)
    def _():
        xn_ref[...] = (_rms(x_ref[...]) * g_ref[...]).astype(_BF16)

    acc = jnp.dot(xn_ref[...], w_ref[...], preferred_element_type=_F32)
    tn = acc.shape[1]

    def emit(val, lo, hi):
        for o_ref, (olo, ohi) in zip(o_refs, out_ranges):
            if olo >= hi or ohi <= lo:
                continue
            if olo <= lo and hi <= ohi:
                o_ref[...] = val.astype(o_ref.dtype)
            else:
                @pl.when((j >= olo) & (j < ohi))
                def _(o_ref=o_ref):
                    o_ref[...] = val.astype(o_ref.dtype)

    if n_rope > 0:
        @pl.when(j < n_rope)
        def _():
            cos = cos_ref[...]
            sin = sin_ref[...]
            pieces = []
            if half == _LANES:
                for c in range(tn // (2 * _LANES)):
                    x1 = acc[:, 2 * c * _LANES:(2 * c + 1) * _LANES]
                    x2 = acc[:, (2 * c + 1) * _LANES:(2 * c + 2) * _LANES]
                    pieces += [x1 * cos - x2 * sin, x2 * cos + x1 * sin]
            else:
                wr = rot_ref.shape[0]
                cos_w = jnp.concatenate([cos] * (wr // _LANES), axis=1)
                sin_w = jnp.concatenate([sin] * (wr // _LANES), axis=1)
                for c in range(tn // wr):
                    xc = acc[:, c * wr:(c + 1) * wr]
                    rot = jnp.dot(xc.astype(_BF16), rot_ref[...], preferred_element_type=_F32)
                    pieces.append(xc * cos_w + rot * sin_w)
            r = jnp.concatenate(pieces, axis=1)
            scale = jnp.where(j < n_first, scale_first, scale_rest).astype(_F32)
            emit(r * scale, 0, n_rope)

        @pl.when(j >= n_rope)
        def _():
            emit(acc, n_rope, 1 << 30)
    else:
        emit(acc, 0, 1 << 30)


def _proj(x, gain, w, cos, sin, *, outs, half=_LANES, n_rope_cols=0, n_first_cols=0, scale_first=1.0,
          scale_rest=1.0, tm=512, tn=512):
    n, d = x.shape
    d_out = w.shape[1]
    tm = _tile(n, tm)
    tn = _tile(d_out, tn)
    assert n_rope_cols % tn == 0 and n_first_cols % tn == 0
    if cos is None:
        cos = jnp.zeros((tm, _LANES), _F32)
        sin = cos
    r_tiles = cos.shape[0] // tm
    assert cos.shape[0] % tm == 0
    if n_rope_cols > 0 and half != _LANES:
        wr = 2 * _LANES
        dst = jnp.arange(wr)
        partner = jnp.where(dst % (2 * half) < half, dst + half, dst - half)
        rot = (jnp.arange(wr)[:, None] == partner[None, :]).astype(_BF16)
    else:
        rot = jnp.zeros((8, _LANES), _BF16)
    out_ranges, out_specs, out_shapes = [], [], []
    for dt, lo, hi in outs:
        assert lo % tn == 0 and hi % tn == 0
        out_ranges.append((lo // tn, hi // tn))
        out_specs.append(pl.BlockSpec((tm, tn), lambda i, j, lo=lo // tn, hi=hi // tn:
                                      (i, jnp.clip(j - lo, 0, hi - lo - 1))))
        out_shapes.append(jax.ShapeDtypeStruct((n, hi - lo), dt))
    kern = functools.partial(_proj_kernel, half=half, n_rope=n_rope_cols // tn, n_first=n_first_cols // tn,
                             scale_first=scale_first, scale_rest=scale_rest, out_ranges=tuple(out_ranges))

    return pl.pallas_call(
        kern,
        grid=(n // tm, d_out // tn),
        in_specs=[
            pl.BlockSpec((tm, d), lambda i, j: (i, 0)),
            pl.BlockSpec((1, d), lambda i, j: (0, 0)),
            pl.BlockSpec((d, tn), lambda i, j: (0, j)),
            pl.BlockSpec((tm, _LANES), lambda i, j: (i % r_tiles, 0)),
            pl.BlockSpec((tm, _LANES), lambda i, j: (i % r_tiles, 0)),
            pl.BlockSpec(rot.shape, lambda i, j: (0, 0)),
        ],
        out_specs=out_specs,
        out_shape=out_shapes,
        scratch_shapes=[pltpu.VMEM((tm, d), _BF16)],
        compiler_params=_params("parallel", "arbitrary"),
        name="norm_proj",
    )(x, gain.reshape(1, d), w, cos, sin, rot)


def _retention_kernel(q_ref, k_ref, v_ref, g_ref, s0_ref, dm_ref, qd_ref, kd_ref, cd_ref, x_ref, wo_ref, xo_ref, so_ref,
                      s_ref, og_ref):
    n = pl.program_id(1)
    heads, dk, dv = s_ref.shape

    @pl.when(n == 0)
    def _():
        s_ref[...] = s0_ref[0]

    for hh in range(heads):
        q = q_ref[:, hh * dk:(hh + 1) * dk]
        k = k_ref[:, hh * dk:(hh + 1) * dk]
        v = v_ref[:, hh * dv:(hh + 1) * dv]
        sc = lax.dot_general(q, k, _NT, preferred_element_type=_F32)
        p = (sc * dm_ref[hh]).astype(_BF16)
        s_old = s_ref[hh]
        q_dec = (q.astype(_F32) * qd_ref[hh]).astype(_BF16)
        o = jnp.dot(p, v, preferred_element_type=_F32) + jnp.dot(q_dec, s_old.astype(_BF16),
                                                                preferred_element_type=_F32)
        k_dec = (k.astype(_F32) * kd_ref[hh]).astype(_BF16)
        s_ref[hh] = s_old * cd_ref[hh] + lax.dot_general(k_dec, v, _TN, preferred_element_type=_F32)
        g = g_ref[:, hh * dv:(hh + 1) * dv].astype(_F32)
        og_ref[:, hh * dv:(hh + 1) * dv] = (_silu(g) * _rms(o)).astype(og_ref.dtype)

    xo_ref[...] = x_ref[...] + jnp.dot(og_ref[...], wo_ref[...], preferred_element_type=_F32)

    @pl.when(n == pl.num_programs(1) - 1)
    def _():
        so_ref[0] = s_ref[...]


def _retention(qkvg, s0, x, w_out, batch, t, *, block=256):
    d = x.shape[1]
    h = _RET_HEADS
    dk, dv = s0.shape[2], s0.shape[3]
    lb = _tile(t, block)
    nb = t // lb
    assert lb % _CHUNK == 0
    log_g = jnp.log1p(-(2.0 ** (-5.0 - jnp.arange(h, dtype=_F32))))
    i = jnp.arange(lb, dtype=_F32)
    ci = jnp.arange(lb) // _CHUNK
    diff = i[:, None] - i[None, :]
    same = ci[:, None] == ci[None, :]
    before = ci[None, :] < ci[:, None]
    expo = jnp.where(same, jnp.abs(diff), diff)
    dmask = jnp.where((same | before)[None], jnp.exp(expo[None] * log_g[:, None, None]), 0.0).astype(_F32)
    q_dec = jnp.exp((i[None, :] + 1.0) * log_g[:, None])[..., None]
    k_dec = jnp.exp((lb - 1.0 - i)[None, :] * log_g[:, None])[..., None]
    c_dec = jnp.exp(lb * log_g).reshape(h, 1, 1)
    wq, wv = h * dk, h * dv
    assert (2 * wq) % wv == 0
    v_blk = (2 * wq) // wv
    return pl.pallas_call(
        _retention_kernel,
        grid=(batch, nb),
        in_specs=[
            pl.BlockSpec((lb, wq), lambda b, n: (b * nb + n, 0)),
            pl.BlockSpec((lb, wq), lambda b, n: (b * nb + n, 1)),
            pl.BlockSpec((lb, wv), lambda b, n: (b * nb + n, v_blk)),
            pl.BlockSpec((lb, wv), lambda b, n: (b * nb + n, v_blk + 1)),
            pl.BlockSpec((1, h, dk, dv), lambda b, n: (b, 0, 0, 0)),
            pl.BlockSpec((h, lb, lb), lambda b, n: (0, 0, 0)),
            pl.BlockSpec((h, lb, 1), lambda b, n: (0, 0, 0)),
            pl.BlockSpec((h, lb, 1), lambda b, n: (0, 0, 0)),
            pl.BlockSpec((h, 1, 1), lambda b, n: (0, 0, 0)),
            pl.BlockSpec((lb, d), lambda b, n: (b * nb + n, 0)),
            pl.BlockSpec((wv, d), lambda b, n: (0, 0)),
        ],
        out_specs=[
            pl.BlockSpec((lb, d), lambda b, n: (b * nb + n, 0)),
            pl.BlockSpec((1, h, dk, dv), lambda b, n: (b, 0, 0, 0)),
        ],
        out_shape=[
            jax.ShapeDtypeStruct((batch * t, d), _F32),
            jax.ShapeDtypeStruct((batch, h, dk, dv), _F32),
        ],
        scratch_shapes=[pltpu.VMEM((h, dk, dv), _F32), pltpu.VMEM((lb, wv), _BF16)],
        compiler_params=_params("parallel", "arbitrary"),
        name="retention",
    )(qkvg, qkvg, qkvg, qkvg, s0, dmask, q_dec, k_dec, c_dec, x, w_out)


def _route_top2(xn, w_router_t):
    logits = lax.dot_general(w_router_t, xn, _NT, preferred_element_type=_F32, precision=lax.Precision.HIGHEST)
    ne = logits.shape[0]
    row = lax.broadcasted_iota(jnp.int32, logits.shape, 0)
    v1 = jnp.max(logits, axis=0, keepdims=True)
    i1 = jnp.min(jnp.where(logits == v1, row, ne), axis=0, keepdims=True)
    rest = jnp.where(row == i1, -jnp.inf, logits)
    v2 = jnp.max(rest, axis=0, keepdims=True)
    i2 = jnp.min(jnp.where(rest == v2, row, ne), axis=0, keepdims=True)
    e2 = jnp.exp(v2 - v1)
    w1 = 1.0 / (1.0 + e2)
    return jnp.concatenate([i1, i2], axis=0), jnp.concatenate([w1, e2 * w1], axis=0)


def _cross_kernel(*refs, heads, pre, route):
    it = iter(refs)
    x_ref = next(it)
    a_ref, wp_ref = (next(it), next(it)) if pre else (None, None)
    g_ref, wq_ref, wo_ref, mk_ref, mv_ref = (next(it) for _ in range(5))
    gr_ref, wr_ref = (next(it), next(it)) if route else (None, None)
    o_ref = next(it)
    x = x_ref[...]
    if pre:
        x = x + jnp.dot(a_ref[...], wp_ref[...], preferred_element_type=_F32)
    d = x.shape[1]
    hd = d // heads
    xn = (_rms(x) * g_ref[...]).astype(_BF16)
    q = (jnp.dot(xn, wq_ref[...], preferred_element_type=_F32) * (hd ** -0.5)).astype(_BF16)
    mk = mk_ref[0].astype(_BF16)
    mv = mv_ref[0].astype(_BF16)
    outs = []
    for hh in range(heads):
        sl = slice(hh * hd, (hh + 1) * hd)
        s = lax.dot_general(q[:, sl], mk[:, sl], _NT, preferred_element_type=_F32)
        s = s - jnp.max(s, axis=-1, keepdims=True)
        e = jnp.exp(s)
        p = (e / jnp.sum(e, axis=-1, keepdims=True)).astype(_BF16)
        outs.append(jnp.dot(p, mv[:, sl], preferred_element_type=_F32).astype(_BF16))
    o = jnp.concatenate(outs, axis=1)
    x_new = x + jnp.dot(o, wo_ref[...], preferred_element_type=_F32)
    o_ref[...] = x_new
    if route:
        idx_ref, wgt_ref = next(it), next(it)
        idx, wgt = _route_top2(_rms(x_new) * gr_ref[...], wr_ref[...])
        idx_ref[...] = idx
        wgt_ref[...] = wgt


def _cross_attention(x, gain, wq, wo, mk, mv, batch, t, *, tm=512, pre=None, route=None):
    n, d = x.shape
    tm = _tile(t, tm)
    nt = t // tm
    mem = mk.shape[1]
    row = lambda b, i: (b * nt + i, 0)
    const = lambda b, i: (0, 0)
    args, specs = [x], [pl.BlockSpec((tm, d), row)]
    if pre is not None:
        a, w_pre = pre
        args += [a, w_pre]
        specs += [pl.BlockSpec((tm, a.shape[1]), row), pl.BlockSpec(w_pre.shape, const)]
    args += [gain.reshape(1, d), wq, wo, mk, mv]
    specs += [pl.BlockSpec((1, d), const), pl.BlockSpec((d, d), const), pl.BlockSpec((d, d), const),
              pl.BlockSpec((1, mem, d), lambda b, i: (b, 0, 0)), pl.BlockSpec((1, mem, d), lambda b, i: (b, 0, 0))]
    out_specs = [pl.BlockSpec((tm, d), row)]
    out_shape = [jax.ShapeDtypeStruct((n, d), _F32)]
    if route is not None:
        gain_r, w_router_t = route
        args += [gain_r.reshape(1, d), w_router_t]
        specs += [pl.BlockSpec((1, d), const), pl.BlockSpec(w_router_t.shape, const)]
        out_specs += [pl.BlockSpec((_TOP_K, tm), lambda b, i: (0, b * nt + i))] * 2
        out_shape += [jax.ShapeDtypeStruct((_TOP_K, n), jnp.int32), jax.ShapeDtypeStruct((_TOP_K, n), _F32)]
    return pl.pallas_call(
        functools.partial(_cross_kernel, heads=_CROSS_HEADS, pre=pre is not None, route=route is not None),
        grid=(batch, nt),
        in_specs=specs,
        out_specs=out_specs,
        out_shape=out_shape,
        compiler_params=_params("parallel", "parallel"),
        name="cross_attention",
    )(*args)


def _ffn_kernel(x_ref, g_ref, wg_ref, wu_ref, wd_ref, o_ref, xn_ref, acc_ref):
    j = pl.program_id(1)

    @pl.when(j == 0)
    def _():
        xn_ref[...] = (_rms(x_ref[...]) * g_ref[...]).astype(_BF16)
        acc_ref[...] = x_ref[...]

    xn = xn_ref[...]
    gate = jnp.dot(xn, wg_ref[...], preferred_element_type=_F32)
    up = jnp.dot(xn, wu_ref[...], preferred_element_type=_F32)
    act = (_silu(gate) * up).astype(_BF16)
    acc_ref[...] += jnp.dot(act, wd_ref[...], preferred_element_type=_F32)

    @pl.when(j == pl.num_programs(1) - 1)
    def _():
        o_ref[...] = acc_ref[...]


def _dense_ffn(x, gain, w_gu, w_down, *, tm=512, tf=1408):
    n, d = x.shape
    f = w_down.shape[0]
    tm = _tile(n, tm)
    tf = _tile(f, tf)
    nf = f // tf
    return pl.pallas_call(
        _ffn_kernel,
        grid=(n // tm, nf),
        in_specs=[
            pl.BlockSpec((tm, d), lambda i, j: (i, 0)),
            pl.BlockSpec((1, d), lambda i, j: (0, 0)),
            pl.BlockSpec((d, tf), lambda i, j: (0, j)),
            pl.BlockSpec((d, tf), lambda i, j: (0, nf + j)),
            pl.BlockSpec((tf, d), lambda i, j: (j, 0)),
        ],
        out_specs=pl.BlockSpec((tm, d), lambda i, j: (i, 0)),
        out_shape=jax.ShapeDtypeStruct((n, d), _F32),
        scratch_shapes=[pltpu.VMEM((tm, d), _BF16), pltpu.VMEM((tm, d), _F32)],
        compiler_params=_params("parallel", "arbitrary"),
        name="dense_swiglu",
    )(x, gain.reshape(1, d), w_gu, w_gu, w_down)


def _diff_attn_kernel(q_ref, k_ref, v_ref, *rest, p0, tq, tk, lam_init, own_keys):
    if own_keys:
        kd_ref, vd_ref = rest[:2]
        rest = rest[2:]
    else:
        kd_ref, vd_ref = k_ref, v_ref
    lam_ref, sub_ref, o_ref, q2_ref, m_ref, acc_ref, s_ref = rest
    qi = pl.program_id(2)
    hd = _DIFF_HD
    hw = 2 * hd

    q = q_ref[...]
    first = lax.broadcasted_iota(jnp.int32, (1, hw), 1) < hd
    q2_ref[0:tq, :] = jnp.where(first, q, 0.0).astype(_BF16)
    q2_ref[tq:2 * tq, :] = jnp.where(first, 0.0, q).astype(_BF16)
    m_ref[...] = jnp.full(m_ref.shape, _NEG, _F32)
    acc_ref[...] = jnp.zeros(acc_ref.shape, _F32)

    def scores(off, nk, ref=k_ref):
        k_blk = ref[pl.ds(off, nk), :].astype(_BF16)
        return lax.dot_general(q2_ref[...], k_blk, _NT, preferred_element_type=_F32)

    def softmax_pv(s, off, nk, visible, ref=v_ref):
        v_blk = ref[pl.ds(off, nk), :].astype(_BF16)
        v_ext = jnp.concatenate([v_blk, jnp.ones(v_blk.shape, v_blk.dtype)], axis=1)
        if visible is not None:
            s = jnp.where(visible, s, _NEG)
        m_old = m_ref[...]
        m_new = jnp.maximum(m_old, jnp.max(s, axis=-1, keepdims=True))
        alpha = jnp.exp2(m_old - m_new)
        if nk % _LANES == 0:
            p = jnp.exp2(s - jnp.concatenate([m_new] * (nk // _LANES), axis=1))
        else:
            p = jnp.exp2(s - m_new[:, 0:1])
        pv = jnp.dot(p.astype(_BF16), v_ext, preferred_element_type=_F32)
        acc_ref[...] = jnp.concatenate([alpha, alpha], axis=1) * acc_ref[...] + pv
        m_ref[...] = m_new

    q_start = p0 + qi * tq
    n_full = q_start // tk
    r_chunk = lax.broadcasted_iota(jnp.int32, (2 * tq, tq), 0) % tq // _CHUNK
    c_chunk = lax.broadcasted_iota(jnp.int32, (2 * tq, tq), 1) // _CHUNK
    vis = c_chunk <= r_chunk
    diag_off = pl.multiple_of(q_start, tq)

    if tq == tk and not own_keys:
        s_ref[0] = scores(0, tk)

        def pair(j):
            off0 = pl.multiple_of(2 * j * tk, tk)
            off1 = pl.multiple_of(off0 + tk, tk)
            s_ref[1] = scores(off1, tk)
            softmax_pv(s_ref[0], off0, tk, None)
            s_ref[0] = scores(pl.multiple_of(off1 + tk, tk), tk)
            softmax_pv(s_ref[1], off1, tk, None)

        n_pairs = n_full // 2
        group = _ATTN_PAIRS_PER_TRIP

        def body(g, carry):
            for u in range(group):
                pair(g * group + u)
            return carry

        lax.fori_loop(0, n_pairs // group, body, 0)
        done = n_pairs // group * group
        width = group // 2
        while width >= 1:
            @pl.when((n_pairs - done) % (2 * width) >= width)
            def _(done=done, width=width):
                for u in range(width):
                    pair(done + u)
            done = done + jnp.where((n_pairs - done) % (2 * width) >= width, width, 0)
            width //= 2

        @pl.when(n_full % 2 == 1)
        def _():
            s_ref[1] = scores(diag_off, tk)
            softmax_pv(s_ref[0], pl.multiple_of(diag_off - tk, tk), tk, None)
            softmax_pv(s_ref[1], diag_off, tq, vis)

        @pl.when(n_full % 2 == 0)
        def _():
            softmax_pv(s_ref[0], diag_off, tq, vis)
    else:
        def body(i, carry):
            off = pl.multiple_of(i * tk, tk)
            softmax_pv(scores(off, tk), off, tk, None)
            return carry

        lax.fori_loop(0, n_full, body, 0)
        own_off = pl.multiple_of(qi * tq, tq) if own_keys else diag_off
        softmax_pv(scores(own_off, tq, kd_ref), own_off, tq, vis, vd_ref)

    lp = lam_ref[...]
    lam = (jnp.exp(jnp.sum(lp[0:1] * lp[1:2], axis=-1, keepdims=True))
           - jnp.exp(jnp.sum(lp[2:3] * lp[3:4], axis=-1, keepdims=True)) + lam_init)
    acc = acc_ref[...]
    o = acc[0:tq, 0:hw] / acc[0:tq, hw:2 * hw] - lam * (acc[tq:2 * tq, 0:hw] / acc[tq:2 * tq, hw:2 * hw])
    o_ref[...] = (_rms(o) * sub_ref[...] * (1.0 - lam_init)).astype(o_ref.dtype)


def _diff_attention(q_arr, k_arr, v_arr, q_col, k_col, v_col, lam_p, subln, batch, t, p0, lam_init, n_heads, *, tq,
                    tk, past=None):
    hw = 2 * _DIFF_HD
    tq = _tile(t, tq)
    nq = t // tq
    tk = math.gcd(tk, p0, tq) if nq > 1 else math.gcd(tk, p0)
    assert tq % _CHUNK == 0 and tk % _CHUNK == 0 and (p0 > 0) == (past is not None)
    own_keys = past is not None
    kern = functools.partial(_diff_attn_kernel, p0=p0, tq=tq, tk=tk, lam_init=lam_init, own_keys=own_keys)
    own_specs = [pl.BlockSpec((t, hw), lambda b, hh, qi: (b, k_col + hh)),
                 pl.BlockSpec((t, hw), lambda b, hh, qi: (b, v_col + hh))]
    if own_keys:
        key_args = [past[0], past[1], k_arr, v_arr]
        key_specs = [pl.BlockSpec((p0, hw), lambda b, hh, qi: (b, hh)),
                     pl.BlockSpec((p0, hw), lambda b, hh, qi: (b, hh))] + own_specs
    else:
        key_args, key_specs = [k_arr, v_arr], own_specs
    return pl.pallas_call(
        kern,
        grid=(batch, n_heads, nq),
        in_specs=[pl.BlockSpec((tq, hw), lambda b, hh, qi: (b * nq + qi, q_col + hh))] + key_specs + [
            pl.BlockSpec(lam_p.shape, lambda b, hh, qi: (0, 0)),
            pl.BlockSpec((1, hw), lambda b, hh, qi: (0, 0)),
        ],
        out_specs=pl.BlockSpec((tq, hw), lambda b, hh, qi: (b * nq + qi, hh)),
        out_shape=jax.ShapeDtypeStruct((batch * t, n_heads * hw), _BF16),
        scratch_shapes=[
            pltpu.VMEM((2 * tq, hw), _BF16),
            pltpu.VMEM((2 * tq, _LANES), _F32),
            pltpu.VMEM((2 * tq, 2 * hw), _F32),
            pltpu.VMEM((2, 2 * tq, tk) if tq == tk and not own_keys else (2, 8, _LANES), _F32),
        ],
        compiler_params=_params("parallel", "parallel", "arbitrary"),
        name="diff_attention",
    )(q_arr, *key_args, lam_p, subln.reshape(1, hw))


def _cached_attn_kernel(qkv_ref, pk_ref, pv_ref, lam_ref, sub_ref, o_ref, *, tk, lam_init):
    t = qkv_ref.shape[0]
    _, p0, n_heads, hw = pk_ref.shape
    hd = hw // 2
    d = n_heads * hw
    first = lax.broadcasted_iota(jnp.int32, (1, hw), 1) < hd
    r_chunk = lax.broadcasted_iota(jnp.int32, (2 * t, t), 0) % t // _CHUNK
    c_chunk = lax.broadcasted_iota(jnp.int32, (2 * t, t), 1) // _CHUNK
    vis = c_chunk <= r_chunk
    lp = lam_ref[...]
    lam = (jnp.exp(jnp.sum(lp[0:1] * lp[1:2], axis=-1, keepdims=True))
           - jnp.exp(jnp.sum(lp[2:3] * lp[3:4], axis=-1, keepdims=True)) + lam_init)

    def step(q2, m_old, acc, k_blk, v_blk, visible):
        s = lax.dot_general(q2, k_blk, _NT, preferred_element_type=_F32)
        if visible is not None:
            s = jnp.where(visible, s, _NEG)
        m_new = jnp.maximum(m_old, jnp.max(s, axis=-1, keepdims=True))
        alpha = jnp.exp2(m_old - m_new)
        p = jnp.exp2(s - m_new[:, 0:1])
        v_ext = jnp.concatenate([v_blk, jnp.ones(v_blk.shape, v_blk.dtype)], axis=1)
        pv = jnp.dot(p.astype(_BF16), v_ext, preferred_element_type=_F32)
        return m_new, jnp.concatenate([alpha, alpha], axis=1) * acc + pv

    for hh in range(n_heads):
        q = qkv_ref[:, hh * hw:(hh + 1) * hw]
        q2 = jnp.concatenate([jnp.where(first, q, 0.0), jnp.where(first, 0.0, q)], axis=0).astype(_BF16)
        m = jnp.full((2 * t, _LANES), _NEG, _F32)
        acc = jnp.zeros((2 * t, 2 * hw), _F32)
        for off in range(0, p0, tk):
            k_blk = pk_ref[0, off:off + tk, hh, :].astype(_BF16)
            v_blk = pv_ref[0, off:off + tk, hh, :].astype(_BF16)
            m, acc = step(q2, m, acc, k_blk, v_blk, None)
        k_own = qkv_ref[:, d + hh * hw:d + (hh + 1) * hw]
        v_own = qkv_ref[:, 2 * d + hh * hw:2 * d + (hh + 1) * hw]
        m, acc = step(q2, m, acc, k_own, v_own, vis)
        o = acc[0:t, 0:hw] / acc[0:t, hw:2 * hw] - lam * (acc[t:2 * t, 0:hw] / acc[t:2 * t, hw:2 * hw])
        o_ref[:, hh * hw:(hh + 1) * hw] = (_rms(o) * sub_ref[...] * (1.0 - lam_init)).astype(o_ref.dtype)


def _cached_attention(qkv, past_k, past_v, lam_p, subln, batch, t, lam_init, *, tk=512):
    _, p0, n_heads, hw = past_k.shape
    d = n_heads * hw
    tk = math.gcd(tk, p0)
    assert p0 % _CHUNK == 0 and t % 8 == 0
    return pl.pallas_call(
        functools.partial(_cached_attn_kernel, tk=tk, lam_init=lam_init),
        grid=(batch,),
        in_specs=[
            pl.BlockSpec((t, 3 * d), lambda b: (b, 0)),
            pl.BlockSpec((1, p0, n_heads, hw), lambda b: (b, 0, 0, 0)),
            pl.BlockSpec((1, p0, n_heads, hw), lambda b: (b, 0, 0, 0)),
            pl.BlockSpec(lam_p.shape, lambda b: (0, 0)),
            pl.BlockSpec((1, hw), lambda b: (0, 0)),
        ],
        out_specs=pl.BlockSpec((t, d), lambda b: (b, 0)),
        out_shape=jax.ShapeDtypeStruct((batch * t, d), _BF16),
        compiler_params=_params("parallel"),
        name="cached_diff_attention",
    )(qkv, past_k, past_v, lam_p, subln.reshape(1, hw))


def _router_kernel(x_ref, g_ref, wr_ref, idx_ref, wgt_ref):
    idx, wgt = _route_top2(_rms(x_ref[...]) * g_ref[...], wr_ref[...])
    idx_ref[...] = idx
    wgt_ref[...] = wgt


def _router(x, gain, w_router_t, *, tm=512):
    n, d = x.shape
    ne = w_router_t.shape[0]
    tm = _tile(n, tm)
    return pl.pallas_call(
        _router_kernel,
        grid=(n // tm,),
        in_specs=[
            pl.BlockSpec((tm, d), lambda i: (i, 0)),
            pl.BlockSpec((1, d), lambda i: (0, 0)),
            pl.BlockSpec((ne, d), lambda i: (0, 0)),
        ],
        out_specs=[
            pl.BlockSpec((_TOP_K, tm), lambda i: (0, i)),
            pl.BlockSpec((_TOP_K, tm), lambda i: (0, i)),
        ],
        out_shape=[
            jax.ShapeDtypeStruct((_TOP_K, n), jnp.int32),
            jax.ShapeDtypeStruct((_TOP_K, n), _F32),
        ],
        compiler_params=_params("parallel"),
        name="moe_router",
    )(x, gain.reshape(1, d), w_router_t)


_GATHER_UNROLL = 8
_ATTN_PAIRS_PER_TRIP = 8


def _start_row_gather(src_hbm, dst_vmem, idx_at, sem, *, static_rows=False):
    def start(r, c):
        pltpu.make_async_copy(src_hbm.at[pl.ds(idx_at(r), 1), :], dst_vmem.at[pl.ds(r, 1), :], sem).start()
        return c

    if static_rows:
        for r in range(dst_vmem.shape[0]):
            start(r, 0)
    else:
        lax.fori_loop(0, dst_vmem.shape[0], start, 0, unroll=_GATHER_UNROLL)


def _wait_row_gather(src_hbm, dst_vmem, sem):
    pltpu.make_async_copy(src_hbm.at[pl.ds(0, dst_vmem.shape[0]), :], dst_vmem, sem).wait()


def _expert_kernel(be_ref, nu_ref, rt_ref, rt_next_ref, x_hbm, g_ref, wg_ref, wu_ref, wd_ref, o_ref, xg_ref, xn_ref,
                   acc_ref, sem, *, rows_per_step):
    del be_ref
    i = pl.program_id(0)
    j = pl.program_id(1)
    n_used = nu_ref[0]
    active = i < n_used
    slot = lax.rem(i, 2)

    @pl.when(active & (i == 0) & (j == 0))
    def _():
        _start_row_gather(x_hbm, xg_ref.at[0], lambda r: rt_ref[0, 0, r], sem.at[0])

    @pl.when((i <= n_used) & (j == 0))
    def _():
        _wait_row_gather(x_hbm, xg_ref.at[slot], sem.at[slot])

    @pl.when(active & (j == 0))
    def _():
        xn_ref[...] = (_rms(xg_ref[slot]) * g_ref[...]).astype(_BF16)
        acc_ref[...] = jnp.zeros(acc_ref.shape, _F32)

    @pl.when(active)
    def _():
        base = j * rows_per_step
        for c in range(rows_per_step):
            r = base + c
            pltpu.make_async_copy(x_hbm.at[pl.ds(rt_next_ref[0, 0, r], 1), :],
                                  xg_ref.at[1 - slot, pl.ds(r, 1), :], sem.at[1 - slot]).start()
        xn = xn_ref[...]
        gate = jnp.dot(xn, wg_ref[0], preferred_element_type=_F32)
        up = jnp.dot(xn, wu_ref[0], preferred_element_type=_F32)
        act = (_silu(gate) * up).astype(_BF16)
        acc_ref[...] += jnp.dot(act, wd_ref[0], preferred_element_type=_F32)

    last = j == pl.num_programs(1) - 1

    @pl.when(active & last)
    def _():
        o_ref[...] = acc_ref[...]

    @pl.when(jnp.logical_not(active) & last)
    def _():
        o_ref[...] = jnp.zeros(o_ref.shape, _F32)


def _experts(x, gain, w_gu, w_down, row_tok, block_e, n_used, *, bm, tf=1792):
    n, d = x.shape
    f = w_down.shape[1]
    tf = _tile(f, tf)
    nf = f // tf
    n_blocks = block_e.shape[0]
    assert bm % nf == 0

    def col(i, j, nu):
        return jnp.where(i < nu[0], j, nf - 1)

    grid_spec = pltpu.PrefetchScalarGridSpec(
        num_scalar_prefetch=2,
        grid=(n_blocks, nf),
        in_specs=[
            pl.BlockSpec((1, 1, bm), lambda i, j, be, nu: (i, 0, 0), memory_space=pltpu.SMEM),
            pl.BlockSpec((1, 1, bm), lambda i, j, be, nu: (jnp.minimum(i + 1, n_blocks - 1), 0, 0),
                         memory_space=pltpu.SMEM),
            pl.BlockSpec(memory_space=pl.ANY),
            pl.BlockSpec((1, d), lambda i, j, be, nu: (0, 0)),
            pl.BlockSpec((1, d, tf), lambda i, j, be, nu: (be[i], 0, col(i, j, nu))),
            pl.BlockSpec((1, d, tf), lambda i, j, be, nu: (be[i], 0, nf + col(i, j, nu))),
            pl.BlockSpec((1, tf, d), lambda i, j, be, nu: (be[i], col(i, j, nu), 0)),
        ],
        out_specs=pl.BlockSpec((bm, d), lambda i, j, be, nu: (i, 0)),
        scratch_shapes=[
            pltpu.VMEM((2, bm, d), _F32),
            pltpu.VMEM((bm, d), _BF16),
            pltpu.VMEM((bm, d), _F32),
            pltpu.SemaphoreType.DMA((2,)),
        ],
    )
    row_tok_b = row_tok.reshape(n_blocks, 1, bm)
    return pl.pallas_call(
        functools.partial(_expert_kernel, rows_per_step=bm // nf),
        grid_spec=grid_spec,
        out_shape=jax.ShapeDtypeStruct((n_blocks * bm, d), _F32),
        compiler_params=_params("arbitrary", "arbitrary"),
        name="moe_experts",
    )(block_e, n_used, row_tok_b, row_tok_b, x, gain.reshape(1, d), w_gu, w_gu, w_down)


def _combine_kernel(d_ref, d_next_ref, x_ref, w_ref, y_hbm, gf_ref, o_ref, r_ref, sem):
    i = pl.program_id(0)
    slot = lax.rem(i, 2)

    def start(dref, s):
        for kk in range(_TOP_K):
            _start_row_gather(y_hbm, r_ref.at[s, kk], lambda r, kk=kk: dref[0, kk, r], sem.at[s],
                              static_rows=isinstance(s, int))

    @pl.when(i == 0)
    def _():
        start(d_ref, 0)

    for s_next in range(2):
        @pl.when((i + 1 < pl.num_programs(0)) & (slot == 1 - s_next))
        def _(s_next=s_next):
            start(d_next_ref, s_next)

    for kk in range(_TOP_K):
        _wait_row_gather(y_hbm, r_ref.at[slot, kk], sem.at[slot])
    w = w_ref[...]
    x = x_ref[...]
    for kk in range(_TOP_K):
        x = x + w[:, kk:kk + 1] * r_ref[slot, kk]
    o_ref[...] = _rms(x) * gf_ref[...]


def _combine_final(x, dest, wgt, yb, gain_final, *, tm=256):
    n, d = x.shape
    tm = _tile(n, tm)
    nt = n // tm
    dest_b = dest.reshape(nt, tm, _TOP_K).transpose(0, 2, 1)
    return pl.pallas_call(
        _combine_kernel,
        grid=(nt,),
        in_specs=[
            pl.BlockSpec((1, _TOP_K, tm), lambda i: (i, 0, 0), memory_space=pltpu.SMEM),
            pl.BlockSpec((1, _TOP_K, tm), lambda i: (jnp.minimum(i + 1, nt - 1), 0, 0), memory_space=pltpu.SMEM),
            pl.BlockSpec((tm, d), lambda i: (i, 0)),
            pl.BlockSpec((tm, _TOP_K), lambda i: (i, 0)),
            pl.BlockSpec(memory_space=pl.ANY),
            pl.BlockSpec((1, d), lambda i: (0, 0)),
        ],
        out_specs=pl.BlockSpec((tm, d), lambda i: (i, 0)),
        out_shape=jax.ShapeDtypeStruct((n, d), _F32),
        scratch_shapes=[
            pltpu.VMEM((2, _TOP_K, tm, d), _F32),
            pltpu.SemaphoreType.DMA((2,)),
        ],
        compiler_params=_params("arbitrary"),
        name="moe_combine_final_norm",
    )(dest_b, dest_b, x, wgt, yb, gain_final.reshape(1, d))


def _moe_final(x, idx, wgt, gain, w_gu, w_down, gain_final, *, bm):
    n, d = x.shape
    ne = w_gu.shape[0]
    a = n * _TOP_K
    e_flat = idx.T.reshape(a)
    onehot = (e_flat[:, None] == jnp.arange(ne, dtype=jnp.int32)[None, :]).astype(jnp.int32)
    csum = jnp.cumsum(onehot, axis=0)
    rank = jnp.sum(csum * onehot, axis=1) - 1
    counts = csum[-1]
    padded = (counts + bm - 1) // bm * bm
    pad_end = jnp.cumsum(padded)
    pad_start = pad_end - padded
    dest = pad_start[e_flat] + rank
    assert a % bm == 0
    n_blocks = a // bm + ne
    row_tok = jnp.zeros((n_blocks * bm,), jnp.int32).at[dest].set(jnp.arange(a, dtype=jnp.int32) // _TOP_K)
    block_start = jnp.arange(n_blocks, dtype=jnp.int32) * bm
    block_e = jnp.minimum(jnp.sum((block_start[:, None] >= pad_end[None, :]).astype(jnp.int32), axis=1), ne - 1)
    n_used = (pad_end[-1:] // bm).astype(jnp.int32)
    yb = _experts(x, gain, w_gu, w_down, row_tok, block_e, n_used, bm=bm)
    return _combine_final(x, dest.reshape(n, _TOP_K), wgt.T, yb, gain_final)


def _rope_tables(pos, dim, lanes_signed):
    half = dim // 2
    inv = _ROPE_THETA ** (-jnp.arange(half, dtype=_F32) / half)
    ang = pos.astype(_F32)[:, None] * inv[None, :]
    cos, sin = jnp.cos(ang), jnp.sin(ang)
    if not lanes_signed:
        return cos, sin
    reps = _LANES // dim
    cos_l = jnp.tile(jnp.concatenate([cos, cos], axis=1), (1, reps))
    sin_l = jnp.tile(jnp.concatenate([-sin, sin], axis=1), (1, reps))
    return cos_l, sin_l


def _trunk(x, pos, p0, ret_s0, past_k, past_v, mem_k, mem_v, w, *, big):
    b, t, d = x.shape
    n = b * t
    xf = x.reshape(n, d)
    tm = 512 if big else 256
    dk = d // _RET_HEADS
    hd_all = d

    cos, sin = _rope_tables(pos, dk, False)
    if cos.shape[0] % tm != 0:
        cos, sin = jnp.tile(cos, (b, 1)), jnp.tile(sin, (b, 1))
    (qkvg,) = _proj(xf, w["norm_mix"][0], w["ret_w_in"], cos, sin, outs=[(_BF16, 0, w["ret_w_in"].shape[1])],
                    half=dk // 2, n_rope_cols=2 * d, n_first_cols=d, scale_first=1.0, scale_rest=dk ** -0.5,
                    tm=2 * tm, tn=1024)
    xf, ret_state = _retention(qkvg, ret_s0, xf, w["ret_w_out"], b, t)
    (xf,) = _cross_attention(xf, w["norm_cross"][0], w["cross_wq"][0], w["cross_wo"][0],
                             mem_k[0].reshape(b, -1, d), mem_v[0].reshape(b, -1, d), b, t, tm=tm)
    xf = _dense_ffn(xf, w["norm_ffn"][0], w["ffn_w_gu"], w["ffn_w_down"], tm=tm)

    lam_init = 0.8 - 0.6 * math.exp(-0.3 * 1)
    cos, sin = _rope_tables(pos, _DIFF_HD, True)
    if cos.shape[0] % tm != 0:
        cos, sin = jnp.tile(cos, (b, 1)), jnp.tile(sin, (b, 1))
    qkv, k_rows, v_rows = _proj(
        xf, w["norm_mix"][1], w["diff_w_in"], cos, sin,
        outs=[(_BF16, 0, 3 * hd_all), (_F32, hd_all, 2 * hd_all), (_F32, 2 * hd_all, 3 * hd_all)],
        half=_DIFF_HD // 2, n_rope_cols=2 * hd_all, n_first_cols=hd_all,
        scale_first=_DIFF_HD ** -0.5 * math.log2(math.e),
        scale_rest=1.0, tm=tm, tn=1024)
    n_heads = d // (2 * _DIFF_HD)
    if past_k is None:
        o = _diff_attention(qkv, qkv, qkv, 0, n_heads, 2 * n_heads, w["diff_lambda"], w["diff_subln"], b, t, p0,
                            lam_init, n_heads, tq=512, tk=512)
    else:
        o = _cached_attention(qkv, past_k, past_v, w["diff_lambda"], w["diff_subln"], b, t, lam_init)
    w_router_t = w["moe_router"].T
    fuse_route = min(t, tm) % _LANES == 0
    res = _cross_attention(xf, w["norm_cross"][1], w["cross_wq"][1], w["cross_wo"][1],
                           mem_k[1].reshape(b, -1, d), mem_v[1].reshape(b, -1, d), b, t, tm=tm,
                           pre=(o, w["diff_w_out"]),
                           route=(w["norm_ffn"][1], w_router_t) if fuse_route else None)
    if fuse_route:
        xf, idx, wgt = res
    else:
        (xf,) = res
        idx, wgt = _router(xf, w["norm_ffn"][1], w_router_t)
    y = _moe_final(xf, idx, wgt, w["norm_ffn"][1], w["moe_w_gu"], w["moe_w_down"], w["norm_final"],
                   bm=512 if big else 128)
    hw = 2 * _DIFF_HD
    return (y.reshape(b, t, d), ret_state, k_rows.reshape(b, t, n_heads, hw), v_rows.reshape(b, t, n_heads, hw))


def kernel(x_prompt, x_sample, state_ret, cache_diff_k, cache_diff_v, cache_mem_k, cache_mem_v, mem_prompt, norm_mix,
           norm_cross, norm_mem, norm_ffn, norm_final, ret_w_in, ret_w_out, diff_w_in, diff_lambda, diff_subln,
           diff_w_out, cross_wq, cross_wkv, cross_wo, ffn_w_gu, ffn_w_down, moe_router, moe_w_gu, moe_w_down):
    depth = norm_mix.shape[0]
    assert depth == 2, "one retention layer followed by one differential-attention layer"
    bf = lambda a: a.astype(_BF16)
    w = dict(norm_mix=norm_mix, norm_cross=norm_cross, norm_ffn=norm_ffn, norm_final=norm_final,
             ret_w_in=bf(ret_w_in[0]), ret_w_out=bf(ret_w_out[0]), diff_w_in=bf(diff_w_in[0]),
             diff_lambda=diff_lambda[0], diff_subln=diff_subln[0], diff_w_out=bf(diff_w_out[0]),
             cross_wq=bf(cross_wq), cross_wo=bf(cross_wo), ffn_w_gu=bf(ffn_w_gu[0]), ffn_w_down=bf(ffn_w_down[0]),
             moe_router=moe_router[0], moe_w_gu=bf(moe_w_gu[0]), moe_w_down=bf(moe_w_down[0]))

    b_p, t_p, d = x_prompt.shape
    b_s, t_s, _ = x_sample.shape
    mem_len = mem_prompt.shape[1]
    mem_flat = mem_prompt.reshape(b_p * mem_len, d)
    mks, mvs = [], []
    for i in range(depth):
        mk, mv = _proj(mem_flat, norm_mem[i], bf(cross_wkv[i]), None, None, outs=[(_F32, 0, d), (_F32, d, 2 * d)],
                       tm=256)
        mks.append(mk.reshape(b_p, mem_len, _CROSS_HEADS, d // _CROSS_HEADS))
        mvs.append(mv.reshape(b_p, mem_len, _CROSS_HEADS, d // _CROSS_HEADS))
    mem_k_p, mem_v_p = jnp.stack(mks), jnp.stack(mvs)

    dkr = d // _RET_HEADS
    s0 = jnp.zeros((b_p, _RET_HEADS, dkr, 2 * dkr), _F32)
    y_p, rs_p, dk_p, dv_p = _trunk(x_prompt, jnp.arange(t_p, dtype=jnp.int32), 0, s0, None, None, mem_k_p, mem_v_p, w,
                                   big=True)
    n_past = cache_diff_k.shape[2]
    y_s, rs_s, dk_s, dv_s = _trunk(x_sample, n_past + jnp.arange(t_s, dtype=jnp.int32), n_past, state_ret[0],
                                   cache_diff_k[0], cache_diff_v[0], cache_mem_k, cache_mem_v, w, big=False)
    return (y_p, y_s, rs_p[None], dk_p[None], dv_p[None], mem_k_p, mem_v_p, rs_s[None], dk_s[None], dv_s[None])
```

```python
import functools
import math

import jax
import jax.numpy as jnp
from jax import lax
from jax.experimental import pallas as pl
from jax.experimental.pallas import tpu as pltpu

_CHUNK = 64
_ROPE_THETA = 10000.0
_EPS = 1e-6
_RET_HEADS = 4
_DIFF_HD = 64
_CROSS_HEADS = 4
_N_EXPERTS = 8
_TOP_K = 2

_VMEM_LIMIT_V7X = 56 * 1024 * 1024
_LANES = 128

_F32 = jnp.float32
_BF16 = jnp.bfloat16
_NT = (((1,), (1,)), ((), ()))
_TN = (((0,), (0,)), ((), ()))
_NEG = -1e30


def _params(*sem):
    return pltpu.CompilerParams(dimension_semantics=sem, vmem_limit_bytes=_VMEM_LIMIT_V7X)


def _tile(n, pref):
    t = min(n, pref)
    assert n % t == 0, (n, pref)
    return t


def _rms(x):
    return x * lax.rsqrt(jnp.mean(x * x, axis=-1, keepdims=True) + _EPS)


def _silu(x):
    return x * jax.nn.sigmoid(x)


def _proj_kernel(x_ref, g_ref, w_ref, cos_ref, sin_ref, rot_ref, *rest, half, n_rope, n_first, scale_first,
                 scale_rest, out_ranges):
    o_refs, xn_ref = rest[:-1], rest[-1]
    j = pl.program_id(1)

    @pl.when(j == 0)
    def _():
        xn_ref[...] = (_rms(x_ref[...]) * g_ref[...]).astype(_BF16)

    acc = jnp.dot(xn_ref[...], w_ref[...], preferred_element_type=_F32)
    tn = acc.shape[1]

    def emit(val, lo, hi):
        for o_ref, (olo, ohi) in zip(o_refs, out_ranges):
            if olo >= hi or ohi <= lo:
                continue
            if olo <= lo and hi <= ohi:
                o_ref[...] = val.astype(o_ref.dtype)
            else:
                @pl.when((j >= olo) & (j < ohi))
                def _(o_ref=o_ref):
                    o_ref[...] = val.astype(o_ref.dtype)

    if n_rope > 0:
        @pl.when(j < n_rope)
        def _():
            cos = cos_ref[...]
            sin = sin_ref[...]
            pieces = []
            if half == _LANES:
                for c in range(tn // (2 * _LANES)):
                    x1 = acc[:, 2 * c * _LANES:(2 * c + 1) * _LANES]
                    x2 = acc[:, (2 * c + 1) * _LANES:(2 * c + 2) * _LANES]
                    pieces += [x1 * cos - x2 * sin, x2 * cos + x1 * sin]
            else:
                wr = rot_ref.shape[0]
                cos_w = jnp.concatenate([cos] * (wr // _LANES), axis=1)
                sin_w = jnp.concatenate([sin] * (wr // _LANES), axis=1)
                for c in range(tn // wr):
                    xc = acc[:, c * wr:(c + 1) * wr]
                    rot = jnp.dot(xc.astype(_BF16), rot_ref[...], preferred_element_type=_F32)
                    pieces.append(xc * cos_w + rot * sin_w)
            r = jnp.concatenate(pieces, axis=1)
            scale = jnp.where(j < n_first, scale_first, scale_rest).astype(_F32)
            emit(r * scale, 0, n_rope)

        @pl.when(j >= n_rope)
        def _():
            emit(acc, n_rope, 1 << 30)
    else:
        emit(acc, 0, 1 << 30)


def _proj(x, gain, w, cos, sin, *, outs, half=_LANES, n_rope_cols=0, n_first_cols=0, scale_first=1.0,
          scale_rest=1.0, tm=512, tn=512):
    n, d = x.shape
    d_out = w.shape[1]
    tm = _tile(n, tm)
    tn = _tile(d_out, tn)
    assert n_rope_cols % tn == 0 and n_first_cols % tn == 0
    if cos is None:
        cos = jnp.zeros((tm, _LANES), _F32)
        sin = cos
    r_tiles = cos.shape[0] // tm
    assert cos.shape[0] % tm == 0
    if n_rope_cols > 0 and half != _LANES:
        wr = 2 * _LANES
        dst = jnp.arange(wr)
        partner = jnp.where(dst % (2 * half) < half, dst + half, dst - half)
        rot = (jnp.arange(wr)[:, None] == partner[None, :]).astype(_BF16)
    else:
        rot = jnp.zeros((8, _LANES), _BF16)
    out_ranges, out_specs, out_shapes = [], [], []
    for dt, lo, hi in outs:
        assert lo % tn == 0 and hi % tn == 0
        out_ranges.append((lo // tn, hi // tn))
        out_specs.append(pl.BlockSpec((tm, tn), lambda i, j, lo=lo // tn, hi=hi // tn:
                                      (i, jnp.clip(j - lo, 0, hi - lo - 1))))
        out_shapes.append(jax.ShapeDtypeStruct((n, hi - lo), dt))
    kern = functools.partial(_proj_kernel, half=half, n_rope=n_rope_cols // tn, n_first=n_first_cols // tn,
                             scale_first=scale_first, scale_rest=scale_rest, out_ranges=tuple(out_ranges))

    return pl.pallas_call(
        kern,
        grid=(n // tm, d_out // tn),
        in_specs=[
            pl.BlockSpec((tm, d), lambda i, j: (i, 0)),
            pl.BlockSpec((1, d), lambda i, j: (0, 0)),
            pl.BlockSpec((d, tn), lambda i, j: (0, j)),
            pl.BlockSpec((tm, _LANES), lambda i, j: (i % r_tiles, 0)),
            pl.BlockSpec((tm, _LANES), lambda i, j: (i % r_tiles, 0)),
            pl.BlockSpec(rot.shape, lambda i, j: (0, 0)),
        ],
        out_specs=out_specs,
        out_shape=out_shapes,
        scratch_shapes=[pltpu.VMEM((tm, d), _BF16)],
        compiler_params=_params("parallel", "arbitrary"),
        name="norm_proj",
    )(x, gain.reshape(1, d), w, cos, sin, rot)


def _retention_kernel(q_ref, k_ref, v_ref, g_ref, s0_ref, dm_ref, qd_ref, kd_ref, cd_ref, x_ref, wo_ref, xo_ref, so_ref,
                      s_ref, og_ref):
    n = pl.program_id(1)
    heads, dk, dv = s_ref.shape

    @pl.when(n == 0)
    def _():
        s_ref[...] = s0_ref[0]

    for hh in range(heads):
        q = q_ref[:, hh * dk:(hh + 1) * dk]
        k = k_ref[:, hh * dk:(hh + 1) * dk]
        v = v_ref[:, hh * dv:(hh + 1) * dv]
        sc = lax.dot_general(q, k, _NT, preferred_element_type=_F32)
        p = (sc * dm_ref[hh]).astype(_BF16)
        s_old = s_ref[hh]
        q_dec = (q.astype(_F32) * qd_ref[hh]).astype(_BF16)
        o = jnp.dot(p, v, preferred_element_type=_F32) + jnp.dot(q_dec, s_old.astype(_BF16),
                                                                preferred_element_type=_F32)
        k_dec = (k.astype(_F32) * kd_ref[hh]).astype(_BF16)
        s_ref[hh] = s_old * cd_ref[hh] + lax.dot_general(k_dec, v, _TN, preferred_element_type=_F32)
        g = g_ref[:, hh * dv:(hh + 1) * dv].astype(_F32)
        og_ref[:, hh * dv:(hh + 1) * dv] = (_silu(g) * _rms(o)).astype(og_ref.dtype)

    xo_ref[...] = x_ref[...] + jnp.dot(og_ref[...], wo_ref[...], preferred_element_type=_F32)

    @pl.when(n == pl.num_programs(1) - 1)
    def _():
        so_ref[0] = s_ref[...]


def _retention(qkvg, s0, x, w_out, batch, t, *, block=256):
    d = x.shape[1]
    h = _RET_HEADS
    dk, dv = s0.shape[2], s0.shape[3]
    lb = _tile(t, block)
    nb = t // lb
    assert lb % _CHUNK == 0
    log_g = jnp.log1p(-(2.0 ** (-5.0 - jnp.arange(h, dtype=_F32))))
    i = jnp.arange(lb, dtype=_F32)
    ci = jnp.arange(lb) // _CHUNK
    diff = i[:, None] - i[None, :]
    same = ci[:, None] == ci[None, :]
    before = ci[None, :] < ci[:, None]
    expo = jnp.where(same, jnp.abs(diff), diff)
    dmask = jnp.where((same | before)[None], jnp.exp(expo[None] * log_g[:, None, None]), 0.0).astype(_F32)
    q_dec = jnp.exp((i[None, :] + 1.0) * log_g[:, None])[..., None]
    k_dec = jnp.exp((lb - 1.0 - i)[None, :] * log_g[:, None])[..., None]
    c_dec = jnp.exp(lb * log_g).reshape(h, 1, 1)
    wq, wv = h * dk, h * dv
    assert (2 * wq) % wv == 0
    v_blk = (2 * wq) // wv
    return pl.pallas_call(
        _retention_kernel,
        grid=(batch, nb),
        in_specs=[
            pl.BlockSpec((lb, wq), lambda b, n: (b * nb + n, 0)),
            pl.BlockSpec((lb, wq), lambda b, n: (b * nb + n, 1)),
            pl.BlockSpec((lb, wv), lambda b, n: (b * nb + n, v_blk)),
            pl.BlockSpec((lb, wv), lambda b, n: (b * nb + n, v_blk + 1)),
            pl.BlockSpec((1, h, dk, dv), lambda b, n: (b, 0, 0, 0)),
            pl.BlockSpec((h, lb, lb), lambda b, n: (0, 0, 0)),
            pl.BlockSpec((h, lb, 1), lambda b, n: (0, 0, 0)),
            pl.BlockSpec((h, lb, 1), lambda b, n: (0, 0, 0)),
            pl.BlockSpec((h, 1, 1), lambda b, n: (0, 0, 0)),
            pl.BlockSpec((lb, d), lambda b, n: (b * nb + n, 0)),
            pl.BlockSpec((wv, d), lambda b, n: (0, 0)),
        ],
        out_specs=[
            pl.BlockSpec((lb, d), lambda b, n: (b * nb + n, 0)),
            pl.BlockSpec((1, h, dk, dv), lambda b, n: (b, 0, 0, 0)),
        ],
        out_shape=[
            jax.ShapeDtypeStruct((batch * t, d), _F32),
            jax.ShapeDtypeStruct((batch, h, dk, dv), _F32),
        ],
        scratch_shapes=[pltpu.VMEM((h, dk, dv), _F32), pltpu.VMEM((lb, wv), _BF16)],
        compiler_params=_params("parallel", "arbitrary"),
        name="retention",
    )(qkvg, qkvg, qkvg, qkvg, s0, dmask, q_dec, k_dec, c_dec, x, w_out)


def _route_top2(xn, w_router_t):
    logits = lax.dot_general(w_router_t, xn, _NT, preferred_element_type=_F32, precision=lax.Precision.HIGHEST)
    ne = logits.shape[0]
    row = lax.broadcasted_iota(jnp.int32, logits.shape, 0)
    v1 = jnp.max(logits, axis=0, keepdims=True)
    i1 = jnp.min(jnp.where(logits == v1, row, ne), axis=0, keepdims=True)
    rest = jnp.where(row == i1, -jnp.inf, logits)
    v2 = jnp.max(rest, axis=0, keepdims=True)
    i2 = jnp.min(jnp.where(rest == v2, row, ne), axis=0, keepdims=True)
    e2 = jnp.exp(v2 - v1)
    w1 = 1.0 / (1.0 + e2)
    return jnp.concatenate([i1, i2], axis=0), jnp.concatenate([w1, e2 * w1], axis=0)


def _cross_kernel(*refs, heads, pre, route):
    it = iter(refs)
    x_ref = next(it)
    a_ref, wp_ref = (next(it), next(it)) if pre else (None, None)
    g_ref, wq_ref, wo_ref, mk_ref, mv_ref = (next(it) for _ in range(5))
    gr_ref, wr_ref = (next(it), next(it)) if route else (None, None)
    o_ref = next(it)
    x = x_ref[...]
    if pre:
        x = x + jnp.dot(a_ref[...], wp_ref[...], preferred_element_type=_F32)
    d = x.shape[1]
    hd = d // heads
    xn = (_rms(x) * g_ref[...]).astype(_BF16)
    q = (jnp.dot(xn, wq_ref[...], preferred_element_type=_F32) * (hd ** -0.5)).astype(_BF16)
    mk = mk_ref[0].astype(_BF16)
    mv = mv_ref[0].astype(_BF16)
    outs = []
    for hh in range(heads):
        sl = slice(hh * hd, (hh + 1) * hd)
        s = lax.dot_general(q[:, sl], mk[:, sl], _NT, preferred_element_type=_F32)
        s = s - jnp.max(s, axis=-1, keepdims=True)
        e = jnp.exp(s)
        p = (e / jnp.sum(e, axis=-1, keepdims=True)).astype(_BF16)
        outs.append(jnp.dot(p, mv[:, sl], preferred_element_type=_F32).astype(_BF16))
    o = jnp.concatenate(outs, axis=1)
    x_new = x + jnp.dot(o, wo_ref[...], preferred_element_type=_F32)
    o_ref[...] = x_new
    if route:
        idx_ref, wgt_ref = next(it), next(it)
        idx, wgt = _route_top2(_rms(x_new) * gr_ref[...], wr_ref[...])
        idx_ref[...] = idx
        wgt_ref[...] = wgt


def _cross_attention(x, gain, wq, wo, mk, mv, batch, t, *, tm=512, pre=None, route=None):
    n, d = x.shape
    tm = _tile(t, tm)
    nt = t // tm
    mem = mk.shape[1]
    row = lambda b, i: (b * nt + i, 0)
    const = lambda b, i: (0, 0)
    args, specs = [x], [pl.BlockSpec((tm, d), row)]
    if pre is not None:
        a, w_pre = pre
        args += [a, w_pre]
        specs += [pl.BlockSpec((tm, a.shape[1]), row), pl.BlockSpec(w_pre.shape, const)]
    args += [gain.reshape(1, d), wq, wo, mk, mv]
    specs += [pl.BlockSpec((1, d), const), pl.BlockSpec((d, d), const), pl.BlockSpec((d, d), const),
              pl.BlockSpec((1, mem, d), lambda b, i: (b, 0, 0)), pl.BlockSpec((1, mem, d), lambda b, i: (b, 0, 0))]
    out_specs = [pl.BlockSpec((tm, d), row)]
    out_shape = [jax.ShapeDtypeStruct((n, d), _F32)]
    if route is not None:
        gain_r, w_router_t = route
        args += [gain_r.reshape(1, d), w_router_t]
        specs += [pl.BlockSpec((1, d), const), pl.BlockSpec(w_router_t.shape, const)]
        out_specs += [pl.BlockSpec((_TOP_K, tm), lambda b, i: (0, b * nt + i))] * 2
        out_shape += [jax.ShapeDtypeStruct((_TOP_K, n), jnp.int32), jax.ShapeDtypeStruct((_TOP_K, n), _F32)]
    return pl.pallas_call(
        functools.partial(_cross_kernel, heads=_CROSS_HEADS, pre=pre is not None, route=route is not None),
        grid=(batch, nt),
        in_specs=specs,
        out_specs=out_specs,
        out_shape=out_shape,
        compiler_params=_params("parallel", "parallel"),
        name="cross_attention",
    )(*args)


def _ffn_kernel(x_ref, g_ref, wg_ref, wu_ref, wd_ref, o_ref, xn_ref, acc_ref):
    j = pl.program_id(1)

    @pl.when(j == 0)
    def _():
        xn_ref[...] = (_rms(x_ref[...]) * g_ref[...]).astype(_BF16)
        acc_ref[...] = x_ref[...]

    xn = xn_ref[...]
    gate = jnp.dot(xn, wg_ref[...], preferred_element_type=_F32)
    up = jnp.dot(xn, wu_ref[...], preferred_element_type=_F32)
    act = (_silu(gate) * up).astype(_BF16)
    acc_ref[...] += jnp.dot(act, wd_ref[...], preferred_element_type=_F32)

    @pl.when(j == pl.num_programs(1) - 1)
    def _():
        o_ref[...] = acc_ref[...]


def _dense_ffn(x, gain, w_gu, w_down, *, tm=512, tf=1408):
    n, d = x.shape
    f = w_down.shape[0]
    tm = _tile(n, tm)
    tf = _tile(f, tf)
    nf = f // tf
    return pl.pallas_call(
        _ffn_kernel,
        grid=(n // tm, nf),
        in_specs=[
            pl.BlockSpec((tm, d), lambda i, j: (i, 0)),
            pl.BlockSpec((1, d), lambda i, j: (0, 0)),
            pl.BlockSpec((d, tf), lambda i, j: (0, j)),
            pl.BlockSpec((d, tf), lambda i, j: (0, nf + j)),
            pl.BlockSpec((tf, d), lambda i, j: (j, 0)),
        ],
        out_specs=pl.BlockSpec((tm, d), lambda i, j: (i, 0)),
        out_shape=jax.ShapeDtypeStruct((n, d), _F32),
        scratch_shapes=[pltpu.VMEM((tm, d), _BF16), pltpu.VMEM((tm, d), _F32)],
        compiler_params=_params("parallel", "arbitrary"),
        name="dense_swiglu",
    )(x, gain.reshape(1, d), w_gu, w_gu, w_down)


def _diff_attn_kernel(q_ref, k_ref, v_ref, *rest, p0, tq, tk, lam_init, own_keys):
    if own_keys:
        kd_ref, vd_ref = rest[:2]
        rest = rest[2:]
    else:
        kd_ref, vd_ref = k_ref, v_ref
    lam_ref, sub_ref, o_ref, q2_ref, m_ref, acc_ref, s_ref = rest
    qi = pl.program_id(2)
    hd = _DIFF_HD
    hw = 2 * hd

    q = q_ref[...]
    first = lax.broadcasted_iota(jnp.int32, (1, hw), 1) < hd
    q2_ref[0:tq, :] = jnp.where(first, q, 0.0).astype(_BF16)
    q2_ref[tq:2 * tq, :] = jnp.where(first, 0.0, q).astype(_BF16)
    m_ref[...] = jnp.full(m_ref.shape, _NEG, _F32)
    acc_ref[...] = jnp.zeros(acc_ref.shape, _F32)

    def scores(off, nk, ref=k_ref):
        k_blk = ref[pl.ds(off, nk), :].astype(_BF16)
        return lax.dot_general(q2_ref[...], k_blk, _NT, preferred_element_type=_F32)

    def softmax_pv(s, off, nk, visible, ref=v_ref):
        v_blk = ref[pl.ds(off, nk), :].astype(_BF16)
        v_ext = jnp.concatenate([v_blk, jnp.ones(v_blk.shape, v_blk.dtype)], axis=1)
        if visible is not None:
            s = jnp.where(visible, s, _NEG)
        m_old = m_ref[...]
        m_new = jnp.maximum(m_old, jnp.max(s, axis=-1, keepdims=True))
        alpha = jnp.exp2(m_old - m_new)
        if nk % _LANES == 0:
            p = jnp.exp2(s - jnp.concatenate([m_new] * (nk // _LANES), axis=1))
        else:
            p = jnp.exp2(s - m_new[:, 0:1])
        pv = jnp.dot(p.astype(_BF16), v_ext, preferred_element_type=_F32)
        acc_ref[...] = jnp.concatenate([alpha, alpha], axis=1) * acc_ref[...] + pv
        m_ref[...] = m_new

    q_start = p0 + qi * tq
    n_full = q_start // tk
    r_chunk = lax.broadcasted_iota(jnp.int32, (2 * tq, tq), 0) % tq // _CHUNK
    c_chunk = lax.broadcasted_iota(jnp.int32, (2 * tq, tq), 1) // _CHUNK
    vis = c_chunk <= r_chunk
    diag_off = pl.multiple_of(q_start, tq)

    if tq == tk and not own_keys:
        s_ref[0] = scores(0, tk)

        def pair(j):
            off0 = pl.multiple_of(2 * j * tk, tk)
            off1 = pl.multiple_of(off0 + tk, tk)
            s_ref[1] = scores(off1, tk)
            softmax_pv(s_ref[0], off0, tk, None)
            s_ref[0] = scores(pl.multiple_of(off1 + tk, tk), tk)
            softmax_pv(s_ref[1], off1, tk, None)

        n_pairs = n_full // 2
        group = _ATTN_PAIRS_PER_TRIP

        def body(g, carry):
            for u in range(group):
                pair(g * group + u)
            return carry

        lax.fori_loop(0, n_pairs // group, body, 0)
        done = n_pairs // group * group
        width = group // 2
        while width >= 1:
            @pl.when((n_pairs - done) % (2 * width) >= width)
            def _(done=done, width=width):
                for u in range(width):
                    pair(done + u)
            done = done + jnp.where((n_pairs - done) % (2 * width) >= width, width, 0)
            width //= 2

        @pl.when(n_full % 2 == 1)
        def _():
            s_ref[1] = scores(diag_off, tk)
            softmax_pv(s_ref[0], pl.multiple_of(diag_off - tk, tk), tk, None)
            softmax_pv(s_ref[1], diag_off, tq, vis)

        @pl.when(n_full % 2 == 0)
        def _():
            softmax_pv(s_ref[0], diag_off, tq, vis)
    else:
        def body(i, carry):
            off = pl.multiple_of(i * tk, tk)
            softmax_pv(scores(off, tk), off, tk, None)
            return carry

        lax.fori_loop(0, n_full, body, 0)
        own_off = pl.multiple_of(qi * tq, tq) if own_keys else diag_off
        softmax_pv(scores(own_off, tq, kd_ref), own_off, tq, vis, vd_ref)

    lp = lam_ref[...]
    lam = (jnp.exp(jnp.sum(lp[0:1] * lp[1:2], axis=-1, keepdims=True))
           - jnp.exp(jnp.sum(lp[2:3] * lp[3:4], axis=-1, keepdims=True)) + lam_init)
    acc = acc_ref[...]
    o = acc[0:tq, 0:hw] / acc[0:tq, hw:2 * hw] - lam * (acc[tq:2 * tq, 0:hw] / acc[tq:2 * tq, hw:2 * hw])
    o_ref[...] = (_rms(o) * sub_ref[...] * (1.0 - lam_init)).astype(o_ref.dtype)


def _diff_attention(q_arr, k_arr, v_arr, q_col, k_col, v_col, lam_p, subln, batch, t, p0, lam_init, n_heads, *, tq,
                    tk, past=None):
    hw = 2 * _DIFF_HD
    tq = _tile(t, tq)
    nq = t // tq
    tk = math.gcd(tk, p0, tq) if nq > 1 else math.gcd(tk, p0)
    assert tq % _CHUNK == 0 and tk % _CHUNK == 0 and (p0 > 0) == (past is not None)
    own_keys = past is not None
    kern = functools.partial(_diff_attn_kernel, p0=p0, tq=tq, tk=tk, lam_init=lam_init, own_keys=own_keys)
    own_specs = [pl.BlockSpec((t, hw), lambda b, hh, qi: (b, k_col + hh)),
                 pl.BlockSpec((t, hw), lambda b, hh, qi: (b, v_col + hh))]
    if own_keys:
        key_args = [past[0], past[1], k_arr, v_arr]
        key_specs = [pl.BlockSpec((p0, hw), lambda b, hh, qi: (b, hh)),
                     pl.BlockSpec((p0, hw), lambda b, hh, qi: (b, hh))] + own_specs
    else:
        key_args, key_specs = [k_arr, v_arr], own_specs
    return pl.pallas_call(
        kern,
        grid=(batch, n_heads, nq),
        in_specs=[pl.BlockSpec((tq, hw), lambda b, hh, qi: (b * nq + qi, q_col + hh))] + key_specs + [
            pl.BlockSpec(lam_p.shape, lambda b, hh, qi: (0, 0)),
            pl.BlockSpec((1, hw), lambda b, hh, qi: (0, 0)),
        ],
        out_specs=pl.BlockSpec((tq, hw), lambda b, hh, qi: (b * nq + qi, hh)),
        out_shape=jax.ShapeDtypeStruct((batch * t, n_heads * hw), _BF16),
        scratch_shapes=[
            pltpu.VMEM((2 * tq, hw), _BF16),
            pltpu.VMEM((2 * tq, _LANES), _F32),
            pltpu.VMEM((2 * tq, 2 * hw), _F32),
            pltpu.VMEM((2, 2 * tq, tk) if tq == tk and not own_keys else (2, 8, _LANES), _F32),
        ],
        compiler_params=_params("parallel", "parallel", "arbitrary"),
        name="diff_attention",
    )(q_arr, *key_args, lam_p, subln.reshape(1, hw))


def _cached_attn_kernel(qkv_ref, pk_ref, pv_ref, lam_ref, sub_ref, o_ref, *, tk, lam_init):
    t = qkv_ref.shape[0]
    _, p0, n_heads, hw = pk_ref.shape
    hd = hw // 2
    d = n_heads * hw
    first = lax.broadcasted_iota(jnp.int32, (1, hw), 1) < hd
    r_chunk = lax.broadcasted_iota(jnp.int32, (2 * t, t), 0) % t // _CHUNK
    c_chunk = lax.broadcasted_iota(jnp.int32, (2 * t, t), 1) // _CHUNK
    vis = c_chunk <= r_chunk
    lp = lam_ref[...]
    lam = (jnp.exp(jnp.sum(lp[0:1] * lp[1:2], axis=-1, keepdims=True))
           - jnp.exp(jnp.sum(lp[2:3] * lp[3:4], axis=-1, keepdims=True)) + lam_init)

    def step(q2, m_old, acc, k_blk, v_blk, visible):
        s = lax.dot_general(q2, k_blk, _NT, preferred_element_type=_F32)
        if visible is not None:
            s = jnp.where(visible, s, _NEG)
        m_new = jnp.maximum(m_old, jnp.max(s, axis=-1, keepdims=True))
        alpha = jnp.exp2(m_old - m_new)
        p = jnp.exp2(s - m_new[:, 0:1])
        v_ext = jnp.concatenate([v_blk, jnp.ones(v_blk.shape, v_blk.dtype)], axis=1)
        pv = jnp.dot(p.astype(_BF16), v_ext, preferred_element_type=_F32)
        return m_new, jnp.concatenate([alpha, alpha], axis=1) * acc + pv

    for hh in range(n_heads):
        q = qkv_ref[:, hh * hw:(hh + 1) * hw]
        q2 = jnp.concatenate([jnp.where(first, q, 0.0), jnp.where(first, 0.0, q)], axis=0).astype(_BF16)
        m = jnp.full((2 * t, _LANES), _NEG, _F32)
        acc = jnp.zeros((2 * t, 2 * hw), _F32)
        for off in range(0, p0, tk):
            k_blk = pk_ref[0, off:off + tk, hh, :].astype(_BF16)
            v_blk = pv_ref[0, off:off + tk, hh, :].astype(_BF16)
            m, acc = step(q2, m, acc, k_blk, v_blk, None)
        k_own = qkv_ref[:, d + hh * hw:d + (hh + 1) * hw]
        v_own = qkv_ref[:, 2 * d + hh * hw:2 * d + (hh + 1) * hw]
        m, acc = step(q2, m, acc, k_own, v_own, vis)
        o = acc[0:t, 0:hw] / acc[0:t, hw:2 * hw] - lam * (acc[t:2 * t, 0:hw] / acc[t:2 * t, hw:2 * hw])
        o_ref[:, hh * hw:(hh + 1) * hw] = (_rms(o) * sub_ref[...] * (1.0 - lam_init)).astype(o_ref.dtype)


def _cached_attention(qkv, past_k, past_v, lam_p, subln, batch, t, lam_init, *, tk=512):
    _, p0, n_heads, hw = past_k.shape
    d = n_heads * hw
    tk = math.gcd(tk, p0)
    assert p0 % _CHUNK == 0 and t % 8 == 0
    return pl.pallas_call(
        functools.partial(_cached_attn_kernel, tk=tk, lam_init=lam_init),
        grid=(batch,),
        in_specs=[
            pl.BlockSpec((t, 3 * d), lambda b: (b, 0)),
            pl.BlockSpec((1, p0, n_heads, hw), lambda b: (b, 0, 0, 0)),
            pl.BlockSpec((1, p0, n_heads, hw), lambda b: (b, 0, 0, 0)),
            pl.BlockSpec(lam_p.shape, lambda b: (0, 0)),
            pl.BlockSpec((1, hw), lambda b: (0, 0)),
        ],
        out_specs=pl.BlockSpec((t, d), lambda b: (b, 0)),
        out_shape=jax.ShapeDtypeStruct((batch * t, d), _BF16),
        compiler_params=_params("parallel"),
        name="cached_diff_attention",
    )(qkv, past_k, past_v, lam_p, subln.reshape(1, hw))


def _router_kernel(x_ref, g_ref, wr_ref, idx_ref, wgt_ref):
    idx, wgt = _route_top2(_rms(x_ref[...]) * g_ref[...], wr_ref[...])
    idx_ref[...] = idx
    wgt_ref[...] = wgt


def _router(x, gain, w_router_t, *, tm=512):
    n, d = x.shape
    ne = w_router_t.shape[0]
    tm = _tile(n, tm)
    return pl.pallas_call(
        _router_kernel,
        grid=(n // tm,),
        in_specs=[
            pl.BlockSpec((tm, d), lambda i: (i, 0)),
            pl.BlockSpec((1, d), lambda i: (0, 0)),
            pl.BlockSpec((ne, d), lambda i: (0, 0)),
        ],
        out_specs=[
            pl.BlockSpec((_TOP_K, tm), lambda i: (0, i)),
            pl.BlockSpec((_TOP_K, tm), lambda i: (0, i)),
        ],
        out_shape=[
            jax.ShapeDtypeStruct((_TOP_K, n), jnp.int32),
            jax.ShapeDtypeStruct((_TOP_K, n), _F32),
        ],
        compiler_params=_params("parallel"),
        name="moe_router",
    )(x, gain.reshape(1, d), w_router_t)


_GATHER_UNROLL = 8
_ATTN_PAIRS_PER_TRIP = 8


def _start_row_gather(src_hbm, dst_vmem, idx_at, sem, *, static_rows=False):
    def start(r, c, priority=0):
        pltpu.make_async_copy(src_hbm.at[pl.ds(idx_at(r), 1), :], dst_vmem.at[pl.ds(r, 1), :], sem).start(priority)
        return c

    if static_rows:
        for r in range(dst_vmem.shape[0]):
            start(r, 0, priority=r % 2)
    else:
        lax.fori_loop(0, dst_vmem.shape[0], start, 0, unroll=_GATHER_UNROLL)


def _wait_row_gather(src_hbm, dst_vmem, sem):
    pltpu.make_async_copy(src_hbm.at[pl.ds(0, dst_vmem.shape[0]), :], dst_vmem, sem).wait()


def _expert_kernel(be_ref, nu_ref, rt_ref, rt_next_ref, x_hbm, g_ref, wg_ref, wu_ref, wd_ref, o_ref, xg_ref, xn_ref,
                   acc_ref, sem, *, rows_per_step):
    del be_ref
    i = pl.program_id(0)
    j = pl.program_id(1)
    n_used = nu_ref[0]
    active = i < n_used
    slot = lax.rem(i, 2)

    @pl.when(active & (i == 0) & (j == 0))
    def _():
        _start_row_gather(x_hbm, xg_ref.at[0], lambda r: rt_ref[0, 0, r], sem.at[0])

    @pl.when((i <= n_used) & (j == 0))
    def _():
        _wait_row_gather(x_hbm, xg_ref.at[slot], sem.at[slot])

    @pl.when(active & (j == 0))
    def _():
        xn_ref[...] = (_rms(xg_ref[slot]) * g_ref[...]).astype(_BF16)
        acc_ref[...] = jnp.zeros(acc_ref.shape, _F32)

    @pl.when(active)
    def _():
        base = j * rows_per_step
        for c in range(rows_per_step):
            r = base + c
            pltpu.make_async_copy(x_hbm.at[pl.ds(rt_next_ref[0, 0, r], 1), :],
                                  xg_ref.at[1 - slot, pl.ds(r, 1), :], sem.at[1 - slot]).start(c % 2)
        xn = xn_ref[...]
        gate = jnp.dot(xn, wg_ref[0], preferred_element_type=_F32)
        up = jnp.dot(xn, wu_ref[0], preferred_element_type=_F32)
        act = (_silu(gate) * up).astype(_BF16)
        acc_ref[...] += jnp.dot(act, wd_ref[0], preferred_element_type=_F32)

    last = j == pl.num_programs(1) - 1

    @pl.when(active & last)
    def _():
        o_ref[...] = acc_ref[...]

    @pl.when(jnp.logical_not(active) & last)
    def _():
        o_ref[...] = jnp.zeros(o_ref.shape, _F32)


def _experts(x, gain, w_gu, w_down, row_tok, block_e, n_used, *, bm, tf=1792):
    n, d = x.shape
    f = w_down.shape[1]
    tf = _tile(f, tf)
    nf = f // tf
    n_blocks = block_e.shape[0]
    assert bm % nf == 0

    def col(i, j, nu):
        return jnp.where(i < nu[0], j, nf - 1)

    grid_spec = pltpu.PrefetchScalarGridSpec(
        num_scalar_prefetch=2,
        grid=(n_blocks, nf),
        in_specs=[
            pl.BlockSpec((1, 1, bm), lambda i, j, be, nu: (i, 0, 0), memory_space=pltpu.SMEM),
            pl.BlockSpec((1, 1, bm), lambda i, j, be, nu: (jnp.minimum(i + 1, n_blocks - 1), 0, 0),
                         memory_space=pltpu.SMEM),
            pl.BlockSpec(memory_space=pl.ANY),
            pl.BlockSpec((1, d), lambda i, j, be, nu: (0, 0)),
            pl.BlockSpec((1, d, tf), lambda i, j, be, nu: (be[i], 0, col(i, j, nu))),
            pl.BlockSpec((1, d, tf), lambda i, j, be, nu: (be[i], 0, nf + col(i, j, nu))),
            pl.BlockSpec((1, tf, d), lambda i, j, be, nu: (be[i], col(i, j, nu), 0)),
        ],
        out_specs=pl.BlockSpec((bm, d), lambda i, j, be, nu: (i, 0)),
        scratch_shapes=[
            pltpu.VMEM((2, bm, d), _F32),
            pltpu.VMEM((bm, d), _BF16),
            pltpu.VMEM((bm, d), _F32),
            pltpu.SemaphoreType.DMA((2,)),
        ],
    )
    row_tok_b = row_tok.reshape(n_blocks, 1, bm)
    return pl.pallas_call(
        functools.partial(_expert_kernel, rows_per_step=bm // nf),
        grid_spec=grid_spec,
        out_shape=jax.ShapeDtypeStruct((n_blocks * bm, d), _F32),
        compiler_params=_params("arbitrary", "arbitrary"),
        name="moe_experts",
    )(block_e, n_used, row_tok_b, row_tok_b, x, gain.reshape(1, d), w_gu, w_gu, w_down)


def _combine_kernel(d_ref, d_next_ref, x_ref, w_ref, y_hbm, gf_ref, o_ref, r_ref, sem):
    i = pl.program_id(0)
    slot = lax.rem(i, 2)

    def start(dref, s):
        for kk in range(_TOP_K):
            _start_row_gather(y_hbm, r_ref.at[s, kk], lambda r, kk=kk: dref[0, kk, r], sem.at[s],
                              static_rows=isinstance(s, int))

    @pl.when(i == 0)
    def _():
        start(d_ref, 0)

    for s_next in range(2):
        @pl.when((i + 1 < pl.num_programs(0)) & (slot == 1 - s_next))
        def _(s_next=s_next):
            start(d_next_ref, s_next)

    for kk in range(_TOP_K):
        _wait_row_gather(y_hbm, r_ref.at[slot, kk], sem.at[slot])
    w = w_ref[...]
    x = x_ref[...]
    for kk in range(_TOP_K):
        x = x + w[:, kk:kk + 1] * r_ref[slot, kk]
    o_ref[...] = _rms(x) * gf_ref[...]


def _combine_final(x, dest, wgt, yb, gain_final, *, tm=256):
    n, d = x.shape
    tm = _tile(n, tm)
    nt = n // tm
    dest_b = dest.reshape(nt, tm, _TOP_K).transpose(0, 2, 1)
    return pl.pallas_call(
        _combine_kernel,
        grid=(nt,),
        in_specs=[
            pl.BlockSpec((1, _TOP_K, tm), lambda i: (i, 0, 0), memory_space=pltpu.SMEM),
            pl.BlockSpec((1, _TOP_K, tm), lambda i: (jnp.minimum(i + 1, nt - 1), 0, 0), memory_space=pltpu.SMEM),
            pl.BlockSpec((tm, d), lambda i: (i, 0)),
            pl.BlockSpec((tm, _TOP_K), lambda i: (i, 0)),
            pl.BlockSpec(memory_space=pl.ANY),
            pl.BlockSpec((1, d), lambda i: (0, 0)),
        ],
        out_specs=pl.BlockSpec((tm, d), lambda i: (i, 0)),
        out_shape=jax.ShapeDtypeStruct((n, d), _F32),
        scratch_shapes=[
            pltpu.VMEM((2, _TOP_K, tm, d), _F32),
            pltpu.SemaphoreType.DMA((2,)),
        ],
        compiler_params=_params("arbitrary"),
        name="moe_combine_final_norm",
    )(dest_b, dest_b, x, wgt, yb, gain_final.reshape(1, d))


def _moe_final(x, idx, wgt, gain, w_gu, w_down, gain_final, *, bm):
    n, d = x.shape
    ne = w_gu.shape[0]
    a = n * _TOP_K
    e_flat = idx.T.reshape(a)
    onehot = (e_flat[:, None] == jnp.arange(ne, dtype=jnp.int32)[None, :]).astype(jnp.int32)
    csum = jnp.cumsum(onehot, axis=0)
    rank = jnp.sum(csum * onehot, axis=1) - 1
    counts = csum[-1]
    padded = (counts + bm - 1) // bm * bm
    pad_end = jnp.cumsum(padded)
    pad_start = pad_end - padded
    dest = pad_start[e_flat] + rank
    assert a % bm == 0
    n_blocks = a // bm + ne
    row_tok = jnp.zeros((n_blocks * bm,), jnp.int32).at[dest].set(jnp.arange(a, dtype=jnp.int32) // _TOP_K)
    block_start = jnp.arange(n_blocks, dtype=jnp.int32) * bm
    block_e = jnp.minimum(jnp.sum((block_start[:, None] >= pad_end[None, :]).astype(jnp.int32), axis=1), ne - 1)
    n_used = (pad_end[-1:] // bm).astype(jnp.int32)
    yb = _experts(x, gain, w_gu, w_down, row_tok, block_e, n_used, bm=bm)
    return _combine_final(x, dest.reshape(n, _TOP_K), wgt.T, yb, gain_final)


def _rope_tables(pos, dim, lanes_signed):
    half = dim // 2
    inv = _ROPE_THETA ** (-jnp.arange(half, dtype=_F32) / half)
    ang = pos.astype(_F32)[:, None] * inv[None, :]
    cos, sin = jnp.cos(ang), jnp.sin(ang)
    if not lanes_signed:
        return cos, sin
    reps = _LANES // dim
    cos_l = jnp.tile(jnp.concatenate([cos, cos], axis=1), (1, reps))
    sin_l = jnp.tile(jnp.concatenate([-sin, sin], axis=1), (1, reps))
    return cos_l, sin_l


def _trunk(x, pos, p0, ret_s0, past_k, past_v, mem_k, mem_v, w, *, big):
    b, t, d = x.shape
    n = b * t
    xf = x.reshape(n, d)
    tm = 512 if big else 256
    dk = d // _RET_HEADS
    hd_all = d

    cos, sin = _rope_tables(pos, dk, False)
    if cos.shape[0] % tm != 0:
        cos, sin = jnp.tile(cos, (b, 1)), jnp.tile(sin, (b, 1))
    (qkvg,) = _proj(xf, w["norm_mix"][0], w["ret_w_in"], cos, sin, outs=[(_BF16, 0, w["ret_w_in"].shape[1])],
                    half=dk // 2, n_rope_cols=2 * d, n_first_cols=d, scale_first=1.0, scale_rest=dk ** -0.5,
                    tm=2 * tm, tn=1024)
    xf, ret_state = _retention(qkvg, ret_s0, xf, w["ret_w_out"], b, t)
    (xf,) = _cross_attention(xf, w["norm_cross"][0], w["cross_wq"][0], w["cross_wo"][0],
                             mem_k[0].reshape(b, -1, d), mem_v[0].reshape(b, -1, d), b, t, tm=tm)
    xf = _dense_ffn(xf, w["norm_ffn"][0], w["ffn_w_gu"], w["ffn_w_down"], tm=tm)

    lam_init = 0.8 - 0.6 * math.exp(-0.3 * 1)
    cos, sin = _rope_tables(pos, _DIFF_HD, True)
    if cos.shape[0] % tm != 0:
        cos, sin = jnp.tile(cos, (b, 1)), jnp.tile(sin, (b, 1))
    qkv, k_rows, v_rows = _proj(
        xf, w["norm_mix"][1], w["diff_w_in"], cos, sin,
        outs=[(_BF16, 0, 3 * hd_all), (_F32, hd_all, 2 * hd_all), (_F32, 2 * hd_all, 3 * hd_all)],
        half=_DIFF_HD // 2, n_rope_cols=2 * hd_all, n_first_cols=hd_all,
        scale_first=_DIFF_HD ** -0.5 * math.log2(math.e),
        scale_rest=1.0, tm=tm, tn=1024)
    n_heads = d // (2 * _DIFF_HD)
    if past_k is None:
        o = _diff_attention(qkv, qkv, qkv, 0, n_heads, 2 * n_heads, w["diff_lambda"], w["diff_subln"], b, t, p0,
                            lam_init, n_heads, tq=512, tk=512)
    else:
        o = _cached_attention(qkv, past_k, past_v, w["diff_lambda"], w["diff_subln"], b, t, lam_init)
    w_router_t = w["moe_router"].T
    fuse_route = min(t, tm) % _LANES == 0
    res = _cross_attention(xf, w["norm_cross"][1], w["cross_wq"][1], w["cross_wo"][1],
                           mem_k[1].reshape(b, -1, d), mem_v[1].reshape(b, -1, d), b, t, tm=tm,
                           pre=(o, w["diff_w_out"]),
                           route=(w["norm_ffn"][1], w_router_t) if fuse_route else None)
    if fuse_route:
        xf, idx, wgt = res
    else:
        (xf,) = res
        idx, wgt = _router(xf, w["norm_ffn"][1], w_router_t)
    y = _moe_final(xf, idx, wgt, w["norm_ffn"][1], w["moe_w_gu"], w["moe_w_down"], w["norm_final"],
                   bm=512 if big else 128)
    hw = 2 * _DIFF_HD
    return (y.reshape(b, t, d), ret_state, k_rows.reshape(b, t, n_heads, hw), v_rows.reshape(b, t, n_heads, hw))


def kernel(x_prompt, x_sample, state_ret, cache_diff_k, cache_diff_v, cache_mem_k, cache_mem_v, mem_prompt, norm_mix,
           norm_cross, norm_mem, norm_ffn, norm_final, ret_w_in, ret_w_out, diff_w_in, diff_lambda, diff_subln,
           diff_w_out, cross_wq, cross_wkv, cross_wo, ffn_w_gu, ffn_w_down, moe_router, moe_w_gu, moe_w_down):
    depth = norm_mix.shape[0]
    assert depth == 2, "one retention layer followed by one differential-attention layer"
    bf = lambda a: a.astype(_BF16)
    w = dict(norm_mix=norm_mix, norm_cross=norm_cross, norm_ffn=norm_ffn, norm_final=norm_final,
             ret_w_in=bf(ret_w_in[0]), ret_w_out=bf(ret_w_out[0]), diff_w_in=bf(diff_w_in[0]),
             diff_lambda=diff_lambda[0], diff_subln=diff_subln[0], diff_w_out=bf(diff_w_out[0]),
             cross_wq=bf(cross_wq), cross_wo=bf(cross_wo), ffn_w_gu=bf(ffn_w_gu[0]), ffn_w_down=bf(ffn_w_down[0]),
             moe_router=moe_router[0], moe_w_gu=bf(moe_w_gu[0]), moe_w_down=bf(moe_w_down[0]))

    b_p, t_p, d = x_prompt.shape
    b_s, t_s, _ = x_sample.shape
    mem_len = mem_prompt.shape[1]
    mem_flat = mem_prompt.reshape(b_p * mem_len, d)
    mks, mvs = [], []
    for i in range(depth):
        mk, mv = _proj(mem_flat, norm_mem[i], bf(cross_wkv[i]), None, None, outs=[(_F32, 0, d), (_F32, d, 2 * d)],
                       tm=256)
        mks.append(mk.reshape(b_p, mem_len, _CROSS_HEADS, d // _CROSS_HEADS))
        mvs.append(mv.reshape(b_p, mem_len, _CROSS_HEADS, d // _CROSS_HEADS))
    mem_k_p, mem_v_p = jnp.stack(mks), jnp.stack(mvs)

    dkr = d // _RET_HEADS
    s0 = jnp.zeros((b_p, _RET_HEADS, dkr, 2 * dkr), _F32)
    y_p, rs_p, dk_p, dv_p = _trunk(x_prompt, jnp.arange(t_p, dtype=jnp.int32), 0, s0, None, None, mem_k_p, mem_v_p, w,
                                   big=True)
    n_past = cache_diff_k.shape[2]
    y_s, rs_s, dk_s, dv_s = _trunk(x_sample, n_past + jnp.arange(t_s, dtype=jnp.int32), n_past, state_ret[0],
                                   cache_diff_k[0], cache_diff_v[0], cache_mem_k, cache_mem_v, w, big=False)
    return (y_p, y_s, rs_p[None], dk_p[None], dv_p[None], mem_k_p, mem_v_p, rs_s[None], dk_s[None], dv_s[None])
```
